```python
import math
import jax
import jax.numpy as jnp
from jax import lax
import numpy as np

D_MODEL = 1024
BATCH = 8
SEQ = 8192
DEPTH = 4

GRID_W = 64
CTX_LEN = 256
N_MIXERS = 3
N_A = (DEPTH + 2) // 3
N_B = (DEPTH + 1) // 3
N_C = DEPTH // 3

D_FF = 2816
FFN_CONV = 3

A_HEADS = 16
A_KV_HEADS = 4
A_HEAD_DIM = 64
A_WINDOW = 128
A_BLOCK = 128
ROPE_BASE = 10000.0

B_HEADS = 8
B_HEAD_DIM = 128
B_CONV = 3
B_CHUNK = 64

C_HEADS = 4
C_KEY_DIM = 128
C_VAL_DIM = 256
C_GATE_RANK = 16
C_GATE_TAU = 16.0
C_CHUNK = 64

NORM_EPS = 1e-5
DN_ALPHA = (2 * DEPTH) ** 0.25
DN_BETA = (8 * DEPTH) ** -0.25

kernel_name = 'hybrid_interleaved_swa_gdn_gla_prefix_dit'


def layer_norm(t, g, b):
    tf = t.astype(jnp.float32)
    mu = jnp.mean(tf, -1, keepdims=True)
    var = jnp.mean(jnp.square(tf - mu), -1, keepdims=True)
    return ((tf - mu) * lax.rsqrt(var + NORM_EPS)).astype(t.dtype) * g + b


def rms_norm(t, g):
    tf = t.astype(jnp.float32)
    return (tf * lax.rsqrt(jnp.mean(tf * tf, -1, keepdims=True) + NORM_EPS)).astype(t.dtype) * g


def l2_normalize(t):
    return t * lax.rsqrt(jnp.sum(t * t, -1, keepdims=True) + 1e-6)


def modulate(t, shift, scale):
    return t * (1 + scale) + shift


def dwconv(t, w):
    k = w.shape[0]
    pad = (k - 1) // 2
    return lax.conv_general_dilated(t, w[:, None, :].astype(t.dtype), window_strides=(1,),
                                    padding=[(pad, pad)], dimension_numbers=('NWC', 'WIO', 'NWC'),
                                    feature_group_count=t.shape[-1])


def axial_rope(rows):
    r = jnp.repeat(jnp.arange(rows), GRID_W).astype(jnp.float32)
    col = jnp.tile(jnp.arange(GRID_W), rows).astype(jnp.float32)
    n_freq = A_HEAD_DIM // 4
    inv = jnp.power(ROPE_BASE, -jnp.arange(n_freq, dtype=jnp.float32) / n_freq)
    ang = jnp.concatenate([r[:, None] * inv, col[:, None] * inv], -1)
    return jnp.cos(ang), jnp.sin(ang)


def apply_rope(t, cos, sin):
    half = t.shape[-1] // 2
    shape = (cos.shape[0],) + (1,) * (t.ndim - 3) + (half,)
    cs = cos.reshape(shape).astype(t.dtype)
    sn = sin.reshape(shape).astype(t.dtype)
    t1, t2 = t[..., :half], t[..., half:]
    return jnp.concatenate([t1 * cs - t2 * sn, t1 * sn + t2 * cs], -1)


def window_attention(u, uc, w_qkv, sink, w_o, cos, sin, need_ctx_out):
    bsz, n, _ = u.shape
    n_ctx = uc.shape[1]
    grp = A_HEADS // A_KV_HEADS
    nq = A_HEADS * A_HEAD_DIM
    nkv = A_KV_HEADS * A_HEAD_DIM
    scale = A_HEAD_DIM ** -0.5
    span = A_BLOCK + 2 * A_WINDOW

    def heads_q(t):
        return t.reshape(t.shape[0], t.shape[1], A_KV_HEADS, grp, A_HEAD_DIM)

    def heads_kv(t):
        return t.reshape(t.shape[0], t.shape[1], A_KV_HEADS, A_HEAD_DIM)

    qkv = u @ w_qkv
    q = apply_rope(heads_q(qkv[..., :nq]), cos, sin) * scale
    k = apply_rope(heads_kv(qkv[..., nq:nq + nkv]), cos, sin)
    v = heads_kv(qkv[..., nq + nkv:])
    kvc = uc @ w_qkv[:, nq:]
    kc = heads_kv(kvc[..., :nkv])
    vc = heads_kv(kvc[..., nkv:])
    sink_l = sink.reshape(A_KV_HEADS, grp).astype(jnp.float32)

    kp = jnp.pad(k, ((0, 0), (A_WINDOW, A_WINDOW), (0, 0), (0, 0)))
    vp = jnp.pad(v, ((0, 0), (A_WINDOW, A_WINDOW), (0, 0), (0, 0)))
    rel = jnp.arange(span)[None, :] - jnp.arange(A_BLOCK)[:, None]
    band = (rel >= 0) & (rel <= 2 * A_WINDOW)

    def block(bi):
        start = bi * A_BLOCK
        qb = lax.dynamic_slice_in_dim(q, start, A_BLOCK, axis=1)
        kb = lax.dynamic_slice_in_dim(kp, start, span, axis=1)
        vb = lax.dynamic_slice_in_dim(vp, start, span, axis=1)
        kpos = start - A_WINDOW + jnp.arange(span)
        valid = band & ((kpos >= 0) & (kpos < n))[None, :]
        s_loc = jnp.where(valid, jnp.einsum('bqkgd,bskd->bkgqs', qb, kb).astype(jnp.float32), -jnp.inf)
        s_ctx = jnp.einsum('bqkgd,bskd->bkgqs', qb, kc).astype(jnp.float32)
        s_snk = jnp.broadcast_to(sink_l[None, :, :, None, None], s_ctx.shape[:-1] + (1,))
        p = jax.nn.softmax(jnp.concatenate([s_loc, s_ctx, s_snk], -1), axis=-1).astype(u.dtype)
        return (jnp.einsum('bkgqs,bskd->bqkgd', p[..., :span], vb)
                + jnp.einsum('bkgqs,bskd->bqkgd', p[..., span:span + n_ctx], vc))

    o = lax.map(block, jnp.arange(n // A_BLOCK))
    y = jnp.moveaxis(o, 0, 1).reshape(bsz, n, nq) @ w_o
    yc = None
    if need_ctx_out:
        qc = heads_q(uc @ w_qkv[:, :nq]) * scale
        s = jnp.einsum('bqkgd,bskd->bkgqs', qc, kc).astype(jnp.float32)
        s_snk = jnp.broadcast_to(sink_l[None, :, :, None, None], s.shape[:-1] + (1,))
        p = jax.nn.softmax(jnp.concatenate([s, s_snk], -1), axis=-1).astype(uc.dtype)
        oc = jnp.einsum('bkgqs,bskd->bqkgd', p[..., :n_ctx], vc)
        yc = oc.reshape(bsz, n_ctx, nq) @ w_o
    return y, yc


def gated_delta_chunk_scan(q, k, v, beta, g, state):
    bsz, h, n, dk = q.shape
    dv = v.shape[-1]
    nc = n // B_CHUNK

    def chunk(t):
        return t.reshape((bsz, h, nc, B_CHUNK) + t.shape[3:])

    q, k, v, beta, g = chunk(q), chunk(k), chunk(v), chunk(beta), chunk(g)
    gc = jnp.cumsum(g, axis=-1)
    tri = jnp.tril(jnp.ones((B_CHUNK, B_CHUNK), bool))
    strict = jnp.tril(jnp.ones((B_CHUNK, B_CHUNK), bool), -1)
    decay = jnp.where(tri, jnp.exp(jnp.where(tri, gc[..., :, None] - gc[..., None, :], 0.0)), 0.0)
    kb = k * beta[..., None]
    lmat = jnp.where(strict, jnp.einsum('bhncd,bhnsd->bhncs', kb, k) * decay, 0.0)
    eye = jnp.eye(B_CHUNK, dtype=q.dtype)
    tmat = lax.linalg.triangular_solve(eye + lmat, jnp.broadcast_to(eye, lmat.shape), left_side=True, lower=True)
    u_w = tmat @ (v * beta[..., None])
    w_w = tmat @ (kb * jnp.exp(gc)[..., None])
    a_qk = jnp.einsum('bhncd,bhnsd->bhncs', q, k) * decay
    q_dec = q * jnp.exp(gc)[..., None]
    g_last = gc[..., -1]
    k_dec = k * jnp.exp(g_last[..., None] - gc)[..., None]
    xs = tuple(jnp.moveaxis(t, 2, 0) for t in (q_dec, k_dec, u_w, w_w, a_qk, g_last))

    def step(s, inp):
        qd, kd, uu, ww, aa, gl = inp
        v_new = uu - ww @ s
        o = qd @ s + aa @ v_new
        s = s * jnp.exp(gl)[..., None, None] + jnp.einsum('bhcd,bhce->bhde', kd, v_new)
        return s, o

    state, o = lax.scan(step, state, xs)
    return jnp.moveaxis(o, 0, 2).reshape(bsz, h, n, dv), state


def gla_chunk_scan(q, k, v, log_a, state):
    bsz, h, n, dk = q.shape
    dv = v.shape[-1]
    nc = n // C_CHUNK

    def chunk(t):
        return jnp.moveaxis(t.reshape(bsz, h, nc, C_CHUNK, t.shape[-1]), 2, 0)

    q, k, v, la = chunk(q), chunk(k), chunk(v), chunk(log_a)
    bcum = jnp.cumsum(la, axis=-2)
    tri = jnp.tril(jnp.ones((C_CHUNK, C_CHUNK), bool))[:, :, None]

    def step(s, inp):
        qi, ki, vi, bi = inp
        rel = jnp.exp(jnp.where(tri, bi[:, :, :, None, :] - bi[:, :, None, :, :], -jnp.inf))
        a = jnp.einsum('bhid,bhjd,bhijd->bhij', qi, ki, rel)
        bl = bi[:, :, -1:, :]
        o = (qi * jnp.exp(bi)) @ s + a @ vi
        s = jnp.exp(bl)[:, :, 0, :, None] * s + jnp.einsum('bhjd,bhje->bhde', ki * jnp.exp(bl - bi), vi)
        return s, o

    state, o = lax.scan(step, state, (q, k, v, bcum))
    return jnp.moveaxis(o, 0, 2).reshape(bsz, h, n, dv), state


def bidirectional_scan(scan_fn, fwd_args, bwd_args, s_fwd, s_bwd):
    o_f, st_f = scan_fn(*fwd_args, s_fwd)
    o_b, st_b = scan_fn(*[jnp.flip(t, 2) for t in bwd_args], s_bwd)
    return o_f + jnp.flip(o_b, 2), st_f, st_b


def gated_deltanet(u, uc, w_in, conv_w, a_log, dt_bias, norm_w, w_o, need_ctx_out):
    hk = B_HEADS * B_HEAD_DIM

    def prep(t):
        bsz, n, _ = t.shape
        z = t @ w_in
        qkv = jax.nn.silu(dwconv(z[..., :3 * hk], conv_w)).astype(jnp.float32)

        def heads(a):
            return a.reshape(bsz, n, B_HEADS, B_HEAD_DIM).transpose(0, 2, 1, 3)

        q = l2_normalize(heads(qkv[..., :hk])) * (B_HEAD_DIM ** -0.5)
        k = l2_normalize(heads(qkv[..., hk:2 * hk]))
        v = heads(qkv[..., 2 * hk:])
        gate = z[..., 3 * hk:4 * hk]
        b_raw = z[..., 4 * hk:4 * hk + 2 * B_HEADS].astype(jnp.float32).reshape(bsz, n, 2, B_HEADS)
        a_raw = z[..., 4 * hk + 2 * B_HEADS:].astype(jnp.float32).reshape(bsz, n, 2, B_HEADS)
        beta = jax.nn.sigmoid(b_raw).transpose(2, 0, 3, 1)
        g = (-jnp.exp(a_log.astype(jnp.float32)) * jax.nn.softplus(a_raw + dt_bias.astype(jnp.float32))).transpose(2, 0, 3, 1)
        return q, k, v, beta, g, gate

    def finish(o, gate):
        bsz, _, n, _ = o.shape
        o = rms_norm(o.transpose(0, 2, 1, 3), norm_w.astype(jnp.float32)).astype(gate.dtype)
        o = o * jax.nn.silu(gate.reshape(bsz, n, B_HEADS, B_HEAD_DIM))
        return o.reshape(bsz, n, hk) @ w_o

    qc, kc, vc, betac, gcx, gatec = prep(uc)
    zero = jnp.zeros((uc.shape[0], B_HEADS, B_HEAD_DIM, B_HEAD_DIM), jnp.float32)
    oc, s_f, s_b = bidirectional_scan(gated_delta_chunk_scan, (qc, kc, vc, betac[0], gcx[0]),
                                      (qc, kc, vc, betac[1], gcx[1]), zero, zero)
    q, k, v, beta, g, gate = prep(u)
    o, _, _ = bidirectional_scan(gated_delta_chunk_scan, (q, k, v, beta[0], g[0]),
                                 (q, k, v, beta[1], g[1]), s_f, s_b)
    y = finish(o, gate)
    yc = finish(oc, gatec) if need_ctx_out else None
    return y, yc


def gla_mixer(u, uc, w_in, w_gate2, gate_b, norm_w, w_o, need_ctx_out):
    dk_all = C_HEADS * C_KEY_DIM
    dv_all = C_HEADS * C_VAL_DIM

    def prep(t):
        bsz, n, _ = t.shape
        z = t @ w_in
        f = z.astype(jnp.float32)

        def heads(a, dim):
            return a.reshape(bsz, n, C_HEADS, dim).transpose(0, 2, 1, 3)

        q = heads(f[..., :dk_all], C_KEY_DIM) * (C_KEY_DIM ** -0.5)
        k = heads(f[..., dk_all:2 * dk_all], C_KEY_DIM)
        v = heads(f[..., 2 * dk_all:2 * dk_all + dv_all], C_VAL_DIM)
        gate = z[..., 2 * dk_all + dv_all:2 * dk_all + 2 * dv_all]
        low = f[..., 2 * dk_all + 2 * dv_all:].reshape(bsz, n, 2, C_GATE_RANK)
        logit = jnp.einsum('blzr,zrd->zbld', low, w_gate2.astype(jnp.float32)) + gate_b.astype(jnp.float32)[:, None, None, :]
        log_a = (jax.nn.log_sigmoid(logit) / C_GATE_TAU).reshape(2, bsz, n, C_HEADS, C_KEY_DIM).transpose(0, 1, 3, 2, 4)
        return q, k, v, log_a, gate

    def finish(o, gate):
        bsz, _, n, _ = o.shape
        o = rms_norm(o.transpose(0, 2, 1, 3), norm_w.astype(jnp.float32)).astype(gate.dtype)
        o = o * jax.nn.silu(gate.reshape(bsz, n, C_HEADS, C_VAL_DIM))
        return o.reshape(bsz, n, dv_all) @ w_o

    qc, kc, vc, lac, gatec = prep(uc)
    zero = jnp.zeros((uc.shape[0], C_HEADS, C_KEY_DIM, C_VAL_DIM), jnp.float32)
    oc, s_f, s_b = bidirectional_scan(gla_chunk_scan, (qc, kc, vc, lac[0]), (qc, kc, vc, lac[1]), zero, zero)
    q, k, v, la, gate = prep(u)
    o, _, _ = bidirectional_scan(gla_chunk_scan, (q, k, v, la[0]), (q, k, v, la[1]), s_f, s_b)
    y = finish(o, gate)
    yc = finish(oc, gatec) if need_ctx_out else None
    return y, yc


def conv_glu(t, w_up, conv_w, conv_b, w_down):
    hdn = t @ w_up
    val, gt = hdn[..., :D_FF], hdn[..., D_FF:]
    gt = dwconv(gt, conv_w) + conv_b
    return (jax.nn.silu(gt) * val) @ w_down


def setup_inputs(seed: int = 0) -> dict:
    key = jax.random.key(seed)
    ks = iter(jax.random.split(key, 32))

    def nrm(shape, scale):
        return jax.random.normal(next(ks), shape, jnp.float32) * scale

    d = D_MODEL
    a_qkv = (A_HEADS + 2 * A_KV_HEADS) * A_HEAD_DIM
    a_out = A_HEADS * A_HEAD_DIM
    b_w = B_HEADS * B_HEAD_DIM
    b_in = 4 * b_w + 4 * B_HEADS
    c_k = C_HEADS * C_KEY_DIM
    c_v = C_HEADS * C_VAL_DIM
    c_in = 2 * c_k + 2 * c_v + 2 * C_GATE_RANK
    x = nrm((BATCH, SEQ, d), 1.0)
    c = nrm((BATCH, d), 1.0)
    ctx = nrm((BATCH, CTX_LEN, d), 1.0)
    c_ctx = nrm((d,), 1.0)
    ada_w = nrm((DEPTH, d, 6 * d), d ** -0.5)
    ada_b = nrm((DEPTH, 6 * d), 0.02)
    ln_g = 1.0 + nrm((DEPTH, 2, d), 0.02)
    ln_b = nrm((DEPTH, 2, d), 0.02)
    ffn_w_up = nrm((DEPTH, d, 2 * D_FF), d ** -0.5)
    ffn_conv_w = nrm((DEPTH, FFN_CONV, D_FF), FFN_CONV ** -0.5)
    ffn_conv_b = nrm((DEPTH, D_FF), 0.02)
    ffn_w_down = nrm((DEPTH, D_FF, d), DN_BETA * D_FF ** -0.5)
    attn_w_qkv = nrm((N_A, d, a_qkv), d ** -0.5)
    attn_sink = nrm((N_A, A_HEADS), 0.5)
    attn_w_o = nrm((N_A, a_out, d), DN_BETA * a_out ** -0.5)
    gdn_w_in = nrm((N_B, d, b_in), d ** -0.5)
    gdn_conv_w = nrm((N_B, B_CONV, 3 * b_w), B_CONV ** -0.5)
    gdn_a_log = jnp.log(jax.random.uniform(next(ks), (N_B, 2, B_HEADS), jnp.float32, 1.0, 16.0))
    dt = jnp.exp(jax.random.uniform(next(ks), (N_B, 2, B_HEADS), jnp.float32, math.log(1e-3), math.log(1e-1)))
    gdn_dt_bias = dt + jnp.log(-jnp.expm1(-dt))
    gdn_norm_w = 1.0 + nrm((N_B, B_HEAD_DIM), 0.02)
    gdn_w_o = nrm((N_B, b_w, d), DN_BETA * b_w ** -0.5)
    gla_w_in = nrm((N_C, d, c_in), d ** -0.5)
    gla_w_gate2 = nrm((N_C, 2, C_GATE_RANK, c_k), C_GATE_RANK ** -0.5)
    gla_gate_b = nrm((N_C, 2, c_k), 0.02)
    gla_norm_w = 1.0 + nrm((N_C, C_VAL_DIM), 0.02)
    gla_w_o = nrm((N_C, c_v, d), DN_BETA * c_v ** -0.5)
    return {'x': x, 'c': c, 'ctx': ctx, 'c_ctx': c_ctx, 'ada_w': ada_w, 'ada_b': ada_b,
            'ln_g': ln_g, 'ln_b': ln_b, 'ffn_w_up': ffn_w_up, 'ffn_conv_w': ffn_conv_w,
            'ffn_conv_b': ffn_conv_b, 'ffn_w_down': ffn_w_down, 'attn_w_qkv': attn_w_qkv,
            'attn_sink': attn_sink, 'attn_w_o': attn_w_o, 'gdn_w_in': gdn_w_in, 'gdn_conv_w': gdn_conv_w,
            'gdn_a_log': gdn_a_log, 'gdn_dt_bias': gdn_dt_bias, 'gdn_norm_w': gdn_norm_w, 'gdn_w_o': gdn_w_o,
            'gla_w_in': gla_w_in, 'gla_w_gate2': gla_w_gate2, 'gla_gate_b': gla_gate_b,
            'gla_norm_w': gla_norm_w, 'gla_w_o': gla_w_o}


def reference(x, c, ctx, c_ctx, ada_w, ada_b, ln_g, ln_b, ffn_w_up, ffn_conv_w, ffn_conv_b, ffn_w_down,
              attn_w_qkv, attn_sink, attn_w_o, gdn_w_in, gdn_conv_w, gdn_a_log, gdn_dt_bias, gdn_norm_w, gdn_w_o,
              gla_w_in, gla_w_gate2, gla_gate_b, gla_norm_w, gla_w_o):
    bsz, n, d = x.shape
    rows = n // GRID_W
    cos, sin = axial_rope(rows)
    xc = ctx
    s_lat = jax.nn.silu(c)
    s_ctx = jax.nn.silu(c_ctx)
    for i in range(DEPTH):
        need_ctx_out = i < DEPTH - 1
        mod = (s_lat @ ada_w[i] + ada_b[i]).reshape(bsz, 6, 1, d)
        modc = (s_ctx @ ada_w[i] + ada_b[i]).reshape(6, 1, d)
        u = modulate(x, mod[:, 0], mod[:, 1])
        uc = modulate(xc, modc[0], modc[1])
        kind, slot = i % N_MIXERS, i // N_MIXERS
        if kind == 0:
            y, yc = window_attention(u, uc, attn_w_qkv[slot], attn_sink[slot], attn_w_o[slot], cos, sin, need_ctx_out)
        elif kind == 1:
            y, yc = gated_deltanet(u, uc, gdn_w_in[slot], gdn_conv_w[slot], gdn_a_log[slot], gdn_dt_bias[slot],
                                   gdn_norm_w[slot], gdn_w_o[slot], need_ctx_out)
        else:
            y, yc = gla_mixer(u, uc, gla_w_in[slot], gla_w_gate2[slot], gla_gate_b[slot], gla_norm_w[slot],
                              gla_w_o[slot], need_ctx_out)
        x = layer_norm(DN_ALPHA * x + mod[:, 2] * y, ln_g[i, 0], ln_b[i, 0])
        h = modulate(x, mod[:, 3], mod[:, 4])
        x = layer_norm(DN_ALPHA * x + mod[:, 5] * conv_glu(h, ffn_w_up[i], ffn_conv_w[i], ffn_conv_b[i], ffn_w_down[i]),
                       ln_g[i, 1], ln_b[i, 1])
        if need_ctx_out:
            xc = layer_norm(DN_ALPHA * xc + modc[2] * yc, ln_g[i, 0], ln_b[i, 0])
            hc = modulate(xc, modc[3], modc[4])
            xc = layer_norm(DN_ALPHA * xc + modc[5] * conv_glu(hc, ffn_w_up[i], ffn_conv_w[i], ffn_conv_b[i], ffn_w_down[i]),
                            ln_g[i, 1], ln_b[i, 1])
    return x
```

```python
import functools
import math

import numpy as np
import jax
import jax.numpy as jnp
from jax import lax
from jax.experimental import pallas as pl
from jax.experimental.pallas import tpu as pltpu

F32 = jnp.float32
BF16 = jnp.bfloat16

D_MODEL = 1024
DEPTH = 4
GRID_W = 64
N_MIXERS = 3
D_FF = 2816
A_HEADS = 16
A_KV_HEADS = 4
A_HEAD_DIM = 64
A_WINDOW = 128
A_BLOCK = 128
ROPE_BASE = 10000.0
B_HEADS = 8
B_HEAD_DIM = 128
C_HEADS = 4
C_KEY_DIM = 128
C_VAL_DIM = 256
C_GATE_RANK = 16
C_GATE_TAU = 16.0
CHUNK = 64
NORM_EPS = 1e-5
DN_ALPHA = (2 * DEPTH) ** 0.25

LANES = 128
SUBLANES = 8
V7X_VMEM_BYTES = 64 * 1024 * 1024
MIB = 1024 * 1024

ROW_TILE = 512
FF_CHUNK = 256
NEG_BIG = -1e30


def _cparams(n_axes, vmem_mib):
    return pltpu.CompilerParams(dimension_semantics=("arbitrary",) * n_axes,
                                vmem_limit_bytes=min(vmem_mib * MIB, V7X_VMEM_BYTES - 8 * MIB))


def _row_tile(lseq, target=ROW_TILE):
    t = min(lseq, target)
    while lseq % t:
        t -= SUBLANES
    return t


def _dot(a, b):
    return jnp.dot(a, b, preferred_element_type=F32)


def _dot_nt(a, b):
    return lax.dot_general(a, b, (((1,), (1,)), ((), ())), preferred_element_type=F32)


def _split3(x):
    hi = x.astype(BF16)
    r = x - hi.astype(F32)
    mid = r.astype(BF16)
    lo = (r - mid.astype(F32)).astype(BF16)
    return hi, mid, lo


def _dot_sel(m_bf, x):
    hi, mid, lo = _split3(x)
    return _dot(m_bf, hi) + _dot(m_bf, mid) + _dot(m_bf, lo)


def _silu(x):
    return x * jax.nn.sigmoid(x)


def _softplus(x):
    return jnp.maximum(x, 0.0) + jnp.log1p(jnp.exp(-jnp.abs(x)))


def _layer_norm(t, g, b):
    mu = jnp.mean(t, -1, keepdims=True)
    d = t - mu
    var = jnp.mean(d * d, -1, keepdims=True)
    return d * lax.rsqrt(var + NORM_EPS) * g + b


def _modulated(x, mod_ref, shift_row):
    sh = mod_ref[0, shift_row:shift_row + 1, :]
    sc = mod_ref[0, shift_row + 1:shift_row + 2, :]
    return x * (1.0 + sc) + sh


def _ada_kernel(s_ref, w_ref, b_ref, o_ref):
    s = _silu(s_ref[...])
    o_ref[0] = jnp.dot(s, w_ref[0], precision=lax.Precision.HIGHEST, preferred_element_type=F32) + b_ref[0]


def _ada(cond, ada_w, ada_b):
    depth, d, n = ada_w.shape
    rows = cond.shape[0]
    tn = 1536
    return pl.pallas_call(
        _ada_kernel,
        grid=(depth, n // tn),
        in_specs=[pl.BlockSpec((rows, d), lambda i, j: (0, 0)),
                  pl.BlockSpec((1, d, tn), lambda i, j: (i, 0, j)),
                  pl.BlockSpec((1, 1, tn), lambda i, j: (i, 0, j))],
        out_specs=pl.BlockSpec((1, rows, tn), lambda i, j: (i, 0, j)),
        out_shape=jax.ShapeDtypeStruct((depth, rows, n), F32),
        compiler_params=_cparams(2, 40),
        name="ada",
    )(cond, ada_w, ada_b.reshape(depth, 1, n))


class _Stream:
    def __init__(self, x, mod, nseq, lseq):
        self.x, self.mod, self.nseq, self.lseq = x, mod, nseq, lseq
        self.tm = _row_tile(lseq)
        self.tps = lseq // self.tm
        self.shared_mod = mod.shape[0] == 1

    def mod_spec(self):
        d = self.mod.shape[-1]
        if self.shared_mod:
            return pl.BlockSpec((1, 6, d), lambda i: (0, 0, 0))
        tps = self.tps
        return pl.BlockSpec((1, 6, d), lambda i: (i // tps, 0, 0))

    def row_spec(self, width, col_block=0):
        return pl.BlockSpec((self.tm, width), lambda i: (i, col_block))

    def head_spec(self, heads, width):
        tps = self.tps
        return pl.BlockSpec((1, heads, self.tm, width), lambda i: (i // tps, 0, i % tps, 0))

    def head_shape(self, heads, width, dtype):
        return jax.ShapeDtypeStruct((self.nseq, heads, self.lseq, width), dtype)

    @property
    def rows(self):
        return self.nseq * self.lseq

    @property
    def grid(self):
        return (self.rows // self.tm,)


def _const_spec(shape):
    nd = len(shape)
    return pl.BlockSpec(shape, lambda i: (0,) * nd)


def _attn_proj_kernel(x_ref, mod_ref, w_ref, *refs, rope):
    o_ref = refs[-1]
    u = _modulated(x_ref[...], mod_ref, 0).astype(BF16)
    z = _dot(u, w_ref[...])
    nq = A_HEADS * A_HEAD_DIM // LANES
    nk = A_KV_HEADS * A_HEAD_DIM // LANES
    for j in range(z.shape[1] // LANES):
        t = z[:, j * LANES:(j + 1) * LANES]
        if j < nq + nk and rope:
            c_ref, s1_ref, s2_ref = refs[:3]
            t = (t * c_ref[...] + pltpu.roll(t, LANES - A_HEAD_DIM // 2, 1) * s1_ref[...]
                 + pltpu.roll(t, A_HEAD_DIM // 2, 1) * s2_ref[...])
        if j < nq:
            t = t * (A_HEAD_DIM ** -0.5)
        o_ref[:, j * LANES:(j + 1) * LANES] = t.astype(BF16)


def _attn_proj(st, w_bf, rope_tabs):
    n = w_bf.shape[1]
    in_specs = [st.row_spec(D_MODEL), st.mod_spec(), _const_spec(w_bf.shape)]
    args = [st.x, st.mod, w_bf]
    if rope_tabs is not None:
        tps = st.tps
        for t in rope_tabs:
            in_specs.append(pl.BlockSpec((st.tm, LANES), lambda i: (i % tps, 0)))
            args.append(t)
    return pl.pallas_call(
        functools.partial(_attn_proj_kernel, rope=rope_tabs is not None),
        grid=st.grid, in_specs=in_specs, out_specs=st.row_spec(n),
        out_shape=jax.ShapeDtypeStruct((st.rows, n), BF16),
        compiler_params=_cparams(1, 40), name="attn_proj",
    )(*args)


def _gdn_proj_kernel(x_ref, mod_ref, w_ref, ws_ref, zh_ref, gate_ref, small_ref):
    u = _modulated(x_ref[...], mod_ref, 0).astype(BF16)
    z = _dot(u, w_ref[...])
    nh = zh_ref.shape[1]
    for j in range(nh):
        zh_ref[0, j] = z[:, j * LANES:(j + 1) * LANES].astype(BF16)
    gate_ref[...] = z[:, nh * LANES:].astype(BF16)
    small_ref[...] = _dot(u, ws_ref[...])


def _gdn_proj(st, w_main, w_small):
    hk = B_HEADS * B_HEAD_DIM
    nh = 3 * B_HEADS
    return pl.pallas_call(
        _gdn_proj_kernel, grid=st.grid,
        in_specs=[st.row_spec(D_MODEL), st.mod_spec(), _const_spec(w_main.shape), _const_spec(w_small.shape)],
        out_specs=[st.head_spec(nh, LANES), st.row_spec(hk), st.row_spec(LANES)],
        out_shape=[st.head_shape(nh, LANES, BF16), jax.ShapeDtypeStruct((st.rows, hk), BF16),
                   jax.ShapeDtypeStruct((st.rows, LANES), F32)],
        compiler_params=_cparams(1, 48), name="gdn_proj",
    )(st.x, st.mod, w_main, w_small)


def _gla_proj_kernel(x_ref, mod_ref, w_ref, ws_ref, q_ref, k_ref, v_ref, gate_ref, low_ref):
    u = _modulated(x_ref[...], mod_ref, 0).astype(BF16)
    z = _dot(u, w_ref[...])
    dk, dv = C_HEADS * C_KEY_DIM, C_HEADS * C_VAL_DIM
    for h in range(C_HEADS):
        q_ref[0, h] = (z[:, h * C_KEY_DIM:(h + 1) * C_KEY_DIM] * (C_KEY_DIM ** -0.5)).astype(BF16)
        k_ref[0, h] = z[:, dk + h * C_KEY_DIM:dk + (h + 1) * C_KEY_DIM].astype(BF16)
        v_ref[0, h] = z[:, 2 * dk + h * C_VAL_DIM:2 * dk + (h + 1) * C_VAL_DIM].astype(BF16)
    gate_ref[...] = z[:, 2 * dk + dv:].astype(BF16)
    low_ref[...] = _dot(u, ws_ref[...])


def _gla_proj(st, w_main, w_small):
    dv = C_HEADS * C_VAL_DIM
    return pl.pallas_call(
        _gla_proj_kernel, grid=st.grid,
        in_specs=[st.row_spec(D_MODEL), st.mod_spec(), _const_spec(w_main.shape), _const_spec(w_small.shape)],
        out_specs=[st.head_spec(C_HEADS, C_KEY_DIM), st.head_spec(C_HEADS, C_KEY_DIM),
                   st.head_spec(C_HEADS, C_VAL_DIM), st.row_spec(dv), st.row_spec(LANES)],
        out_shape=[st.head_shape(C_HEADS, C_KEY_DIM, BF16), st.head_shape(C_HEADS, C_KEY_DIM, BF16),
                   st.head_shape(C_HEADS, C_VAL_DIM, BF16), jax.ShapeDtypeStruct((st.rows, dv), BF16),
                   jax.ShapeDtypeStruct((st.rows, LANES), F32)],
        compiler_params=_cparams(1, 48), name="gla_proj",
    )(st.x, st.mod, w_main, w_small)


def _attn_kernel(sink_ref, q_ref, *refs, local, nblk):
    if local:
        kp_ref, kc_ref, kn_ref, vp_ref, vc_ref, vn_ref, kx_ref, vx_ref, o_ref = refs
    else:
        kx_ref, vx_ref, o_ref = refs
    j = pl.program_id(1)
    grp = A_HEADS // A_KV_HEADS
    hd = A_HEAD_DIM
    blk = q_ref.shape[0]
    q = q_ref[...]
    if local:
        span = 3 * blk
        shape = (grp * blk, span)
        qi = lax.broadcasted_iota(jnp.int32, shape, 0) & (blk - 1)
        r = lax.broadcasted_iota(jnp.int32, shape, 1)
        rel = r - qi
        lo = jnp.where(j > 0, 0, blk)
        hi = jnp.where(j < nblk - 1, span, 2 * blk)
        valid = (rel >= 0) & (rel <= 2 * A_WINDOW) & (r >= lo) & (r < hi)
        bias = jnp.where(valid, 0.0, NEG_BIG)
    for h in range(A_KV_HEADS):
        hs = slice(h * hd, (h + 1) * hd)
        qh = jnp.concatenate([q[:, (h * grp + g) * hd:(h * grp + g + 1) * hd] for g in range(grp)], axis=0)
        snk = jnp.concatenate([jnp.full((blk, 1), sink_ref[h * grp + g], F32) for g in range(grp)], axis=0)
        s_ctx = _dot_nt(qh, kx_ref[:, hs])
        m = jnp.maximum(jnp.max(s_ctx, -1, keepdims=True), snk)
        if local:
            kl = jnp.concatenate([kp_ref[:, hs], kc_ref[:, hs], kn_ref[:, hs]], axis=0)
            vl = jnp.concatenate([vp_ref[:, hs], vc_ref[:, hs], vn_ref[:, hs]], axis=0)
            s_loc = _dot_nt(qh, kl) + bias
            m = jnp.maximum(m, jnp.max(s_loc, -1, keepdims=True))
        p_ctx = jnp.exp(s_ctx - m)
        den = jnp.sum(p_ctx, -1, keepdims=True) + jnp.exp(snk - m)
        acc = _dot(p_ctx.astype(BF16), vx_ref[:, hs])
        if local:
            p_loc = jnp.exp(s_loc - m)
            den = den + jnp.sum(p_loc, -1, keepdims=True)
            acc = acc + _dot(p_loc.astype(BF16), vl)
        o = acc / den
        for g in range(grp):
            c0 = (h * grp + g) * hd
            o_ref[:, c0:c0 + hd] = o[g * blk:(g + 1) * blk].astype(BF16)


def _attention(qkv, qkv_ctx, sink, nseq, lseq, lctx, local):
    blk = A_BLOCK
    nblk = lseq // blk
    nq = A_HEADS * A_HEAD_DIM
    nkv = A_KV_HEADS * A_HEAD_DIM
    kcol, vcol = nq // nkv, nq // nkv + 1

    def row(f):
        return lambda b, j: (b * nblk + f(j), 0)

    in_specs = [pl.BlockSpec(memory_space=pltpu.SMEM), pl.BlockSpec((blk, nq), row(lambda j: j))]
    args = [sink, qkv]
    if local:
        for col in (kcol, vcol):
            for f in (lambda j: jnp.maximum(j - 1, 0), lambda j: j, lambda j: jnp.minimum(j + 1, nblk - 1)):
                in_specs.append(pl.BlockSpec((blk, nkv), (lambda f, col: lambda b, j: (b * nblk + f(j), col))(f, col)))
                args.append(qkv)
    for col in (kcol, vcol):
        in_specs.append(pl.BlockSpec((lctx, nkv), (lambda col: lambda b, j: (b, col))(col)))
        args.append(qkv_ctx)
    return pl.pallas_call(
        functools.partial(_attn_kernel, local=local, nblk=nblk),
        grid=(nseq, nblk), in_specs=in_specs,
        out_specs=pl.BlockSpec((blk, nq), row(lambda j: j)),
        out_shape=jax.ShapeDtypeStruct((nseq * lseq, nq), BF16),
        compiler_params=_cparams(2, 32), name="attn_local" if local else "attn_ctx",
    )(*args)


def _gdn_act_kernel(z_ref, zp_ref, zn_ref, small_ref, cw_ref, av_ref, dt_ref, q_ref, k_ref, v_ref, bg_ref, *, tps):
    t = pl.program_id(0) % tps
    has_prev = (t > 0).astype(F32)
    has_next = (t < tps - 1).astype(F32)
    tm = z_ref.shape[2]
    rows = lax.broadcasted_iota(jnp.int32, (tm, LANES), 0)
    first, last = rows == 0, rows == tm - 1
    for j in range(3 * B_HEADS):
        zc = z_ref[0, j].astype(F32)
        prev_row = zp_ref[0, j, SUBLANES - 1:SUBLANES, :].astype(F32) * has_prev
        next_row = zn_ref[0, j, 0:1, :].astype(F32) * has_next
        zp = jnp.where(first, prev_row, pltpu.roll(zc, 1, 0))
        zn = jnp.where(last, next_row, pltpu.roll(zc, tm - 1, 0))
        y = _silu(cw_ref[0, j] * zp + cw_ref[1, j] * zc + cw_ref[2, j] * zn)
        if j < 2 * B_HEADS:
            y = y * lax.rsqrt(jnp.sum(y * y, -1, keepdims=True) + 1e-6)
        if j < B_HEADS:
            q_ref[0, j] = (y * (B_HEAD_DIM ** -0.5)).astype(BF16)
        elif j < 2 * B_HEADS:
            k_ref[0, j - B_HEADS] = y.astype(BF16)
        else:
            v_ref[0, j - 2 * B_HEADS] = y.astype(BF16)
    zs = small_ref[...]
    lane = lax.broadcasted_iota(jnp.int32, zs.shape, 1)
    beta = jax.nn.sigmoid(zs)
    g = -av_ref[...] * _softplus(zs + dt_ref[...])
    bg_ref[...] = jnp.where(lane < 2 * B_HEADS, beta, g)


def _gdn_act(zh, small, conv_w3, a_vec, dt_vec, nseq, lseq):
    tm = _row_tile(lseq)
    tps = lseq // tm
    nh = 3 * B_HEADS
    nb8 = lseq // SUBLANES
    r8 = tm // SUBLANES

    def seq(i):
        return i // tps

    out_spec = pl.BlockSpec((1, B_HEADS, tm, LANES), lambda i: (seq(i), 0, i % tps, 0))
    out_shape = jax.ShapeDtypeStruct((nseq, B_HEADS, lseq, LANES), BF16)
    return pl.pallas_call(
        functools.partial(_gdn_act_kernel, tps=tps),
        grid=(nseq * tps,),
        in_specs=[pl.BlockSpec((1, nh, tm, LANES), lambda i: (seq(i), 0, i % tps, 0)),
                  pl.BlockSpec((1, nh, SUBLANES, LANES), lambda i: (seq(i), 0, jnp.maximum((i % tps) * r8 - 1, 0), 0)),
                  pl.BlockSpec((1, nh, SUBLANES, LANES),
                               lambda i: (seq(i), 0, jnp.minimum((i % tps + 1) * r8, nb8 - 1), 0)),
                  pl.BlockSpec((tm, LANES), lambda i: (i, 0)),
                  _const_spec(conv_w3.shape), _const_spec(a_vec.shape), _const_spec(dt_vec.shape)],
        out_specs=[out_spec, out_spec, out_spec, pl.BlockSpec((tm, LANES), lambda i: (i, 0))],
        out_shape=[out_shape, out_shape, out_shape, jax.ShapeDtypeStruct((nseq * lseq, LANES), F32)],
        compiler_params=_cparams(1, 48), name="gdn_act",
    )(zh, zh, zh, small, conv_w3, a_vec, dt_vec)


def _unit_lower_inverse_minus_identity(lmat, ri, ci):
    b16 = (ri >> 4) == (ci >> 4)
    b32 = (ri >> 5) == (ci >> 5)

    def mm(a, b):
        return _dot(a.astype(BF16), b.astype(BF16))

    l_bd = jnp.where(b16, lmat, 0.0)
    m2 = mm(l_bd, l_bd)
    m4 = mm(m2, m2)
    m8 = mm(m4, m4)
    p = m2 - l_bd - mm(l_bd, m2)
    p = p + m4 + mm(p, m4)
    p = p + m8 + mm(p, m8)
    for lo in (jnp.where(b32 & ~b16, lmat, 0.0), jnp.where(b32, 0.0, lmat)):
        y = lo + mm(p, lo)
        p = p - (y + mm(y, p))
    return p


def _gdn_scan_kernel(qf_ref, kf_ref, vf_ref, bgf_ref, qb_ref, kb_ref, vb_ref, bgb_ref, s0_ref,
                     of_ref, ob_ref, s_ref):
    c = pl.program_id(1)

    @pl.when(c == 0)
    def _():
        s_ref[...] = s0_ref[...]

    n = CHUNK
    ri = lax.broadcasted_iota(jnp.int32, (n, n), 0)
    ci = lax.broadcasted_iota(jnp.int32, (n, n), 1)
    lane = lax.broadcasted_iota(jnp.int32, (n, LANES), 1)
    ones_bf = jnp.ones((n, n), BF16)
    zpad = jnp.zeros((n, LANES), F32)
    dirs = ((qf_ref, kf_ref, vf_ref, bgf_ref, of_ref, ri >= ci, ri <= ci, ri > ci, n - 1),
            (qb_ref, kb_ref, vb_ref, bgb_ref, ob_ref, ri <= ci, ri >= ci, ri < ci, 0))

    def unit(d, h):
        q_ref, k_ref, v_ref, bg_ref, o_ref, incl, incl_t, strict, last = dirs[d]
        q = q_ref[0, h].astype(F32)
        k = k_ref[0, h].astype(F32)
        v = v_ref[0, h].astype(F32)
        bg = bg_ref[0]
        beta = jnp.sum(jnp.where(lane == d * B_HEADS + h, bg, 0.0), -1, keepdims=True)
        g = jnp.sum(jnp.where(lane == (2 + d) * B_HEADS + h, bg, 0.0), -1, keepdims=True)
        g_rep = jnp.broadcast_to(g, (n, LANES))
        gc = _dot_sel(incl.astype(BF16), g_rep)
        gc_row = _dot_sel(ones_bf, jnp.where(incl_t, g_rep[:, :n], 0.0))
        decay = jnp.where(incl, jnp.exp(jnp.where(incl, gc[:, :n] - gc_row, 0.0)), 0.0)
        kbeta = k * beta
        k_bf = k.astype(BF16)
        lmat = jnp.where(strict, _dot_nt(kbeta.astype(BF16), k_bf) * decay, 0.0)
        a_qk = jnp.where(incl, _dot_nt(q.astype(BF16), k_bf) * decay, 0.0)
        tp = _unit_lower_inverse_minus_identity(lmat, ri, ci)
        egc = jnp.exp(gc)
        rhs = jnp.concatenate([v * beta, kbeta * egc], axis=1)
        uw = rhs + _dot(tp.astype(BF16), rhs.astype(BF16))
        gl = gc[last:last + 1, :]
        kdec = k * jnp.exp(gl - gc)
        kdec_t = jnp.concatenate([kdec, zpad], axis=0).T.astype(BF16)
        u = d * B_HEADS + h
        s = s_ref[0, u]
        s_bf = s.astype(BF16)
        v_new = uw[:, :B_HEAD_DIM] - _dot(uw[:, B_HEAD_DIM:].astype(BF16), s_bf)
        vn_bf = v_new.astype(BF16)
        o_ref[0, h] = _dot((q * egc).astype(BF16), s_bf) + _dot(a_qk.astype(BF16), vn_bf)
        vn_pad = jnp.concatenate([vn_bf, jnp.zeros((n, LANES), BF16)], axis=0)
        s_ref[0, u] = s * jnp.exp(gl) + _dot(kdec_t, vn_pad)

    def body(h, carry):
        unit(0, h)
        unit(1, h)
        return carry

    lax.fori_loop(0, B_HEADS, body, 0)


def _gdn_scan(q, k, v, bg, s0, nseq, lseq):
    nc = lseq // CHUNK
    bg3 = bg.reshape(nseq, lseq, LANES)
    hspec_f = pl.BlockSpec((1, B_HEADS, CHUNK, LANES), lambda b, c: (b, 0, c, 0))
    hspec_b = pl.BlockSpec((1, B_HEADS, CHUNK, LANES), lambda b, c: (b, 0, nc - 1 - c, 0))
    gspec_f = pl.BlockSpec((1, CHUNK, LANES), lambda b, c: (b, c, 0))
    gspec_b = pl.BlockSpec((1, CHUNK, LANES), lambda b, c: (b, nc - 1 - c, 0))
    sspec = pl.BlockSpec((1, 2 * B_HEADS, B_HEAD_DIM, B_HEAD_DIM), lambda b, c: (b, 0, 0, 0))
    oshape = jax.ShapeDtypeStruct((nseq, B_HEADS, lseq, LANES), F32)
    return pl.pallas_call(
        _gdn_scan_kernel, grid=(nseq, nc),
        in_specs=[hspec_f, hspec_f, hspec_f, gspec_f, hspec_b, hspec_b, hspec_b, gspec_b, sspec],
        out_specs=[hspec_f, hspec_b, sspec],
        out_shape=[oshape, oshape, jax.ShapeDtypeStruct(s0.shape, F32)],
        compiler_params=_cparams(2, 32), name="gdn_scan",
    )(q, k, v, bg3, q, k, v, bg3, s0)


def _gla_level_mats():
    n = CHUNK
    mats = []
    s = n // 2
    while s >= 1:
        m = np.zeros((n, n), np.float32)
        for i in range(n):
            ref = (i // (2 * s)) * 2 * s + s
            if i >= ref:
                m[i, ref + 1:i + 1] = 1.0
            else:
                m[i, i + 1:ref + 1] = -1.0
        mats.append(m)
        s //= 2
    mats.append(np.tril(np.ones((n, n), np.float32)))
    mats.append(np.triu(np.ones((n, n), np.float32), 1))
    fwd = np.concatenate(mats, 0)
    bwd = np.concatenate([m[::-1, ::-1] for m in mats], 0)
    return fwd, bwd


N_LEVELS = int(math.log2(CHUNK))


def _gla_scan_kernel(qf_ref, kf_ref, vf_ref, lowf_ref, qb_ref, kb_ref, vb_ref, lowb_ref, w2_ref, gb_ref,
                     mf_ref, mb_ref, s0_ref, of_ref, ob_ref, s_ref, lev_ref):
    c = pl.program_id(1)

    @pl.when(c == 0)
    def _():
        s_ref[...] = s0_ref[...]

    n = CHUNK
    dk_all = C_HEADS * C_KEY_DIM
    ri = lax.broadcasted_iota(jnp.int32, (n, n), 0)
    ci = lax.broadcasted_iota(jnp.int32, (n, n), 1)
    dirs = ((qf_ref, kf_ref, vf_ref, lowf_ref, mf_ref, of_ref, n - 1), (qb_ref, kb_ref, vb_ref, lowb_ref, mb_ref, ob_ref, 0))

    for d in range(2):
        low_ref, m_ref = dirs[d][3], dirs[d][4]
        logit = jnp.dot(low_ref[0], w2_ref[:, d * dk_all:(d + 1) * dk_all], precision=lax.Precision.HIGHEST,
                        preferred_element_type=F32) + gb_ref[:, d * dk_all:(d + 1) * dk_all]
        log_a = (jnp.minimum(logit, 0.0) - jnp.log1p(jnp.exp(-jnp.abs(logit)))) * (1.0 / C_GATE_TAU)
        lev = _dot_sel(m_ref[...], log_a)
        for h in range(C_HEADS):
            lev_ref[d * C_HEADS + h] = lev[:, h * C_KEY_DIM:(h + 1) * C_KEY_DIM]

    def unit(d, h):
        q_ref, k_ref, v_ref, _, _, o_ref, last = dirs[d]
        u = d * C_HEADS + h
        q = q_ref[0, h].astype(F32)
        k = k_ref[0, h].astype(F32)
        v_bf = v_ref[0, h]
        a = jnp.where(ri == ci, _dot_nt(q.astype(BF16), k.astype(BF16)), 0.0)
        for lvl in range(N_LEVELS):
            x = jnp.exp(-jnp.abs(lev_ref[u, lvl * n:(lvl + 1) * n, :]))
            sh = N_LEVELS - 1 - lvl
            same = (ri >> (sh + 1)) == (ci >> (sh + 1))
            hi_r, hi_c = ((ri >> sh) & 1) == 1, ((ci >> sh) & 1) == 1
            pair = same & (hi_r & ~hi_c if d == 0 else ~hi_r & hi_c)
            a = a + jnp.where(pair, _dot_nt((q * x).astype(BF16), (k * x).astype(BF16)), 0.0)
        bcum = lev_ref[u, N_LEVELS * n:(N_LEVELS + 1) * n, :]
        rest = lev_ref[u, (N_LEVELS + 1) * n:(N_LEVELS + 2) * n, :]
        st = s_ref[0, u]
        o_ref[0, h] = _dot_nt((q * jnp.exp(bcum)).astype(BF16), st.astype(BF16)) + _dot(a.astype(BF16), v_bf)
        kdec = (k * jnp.exp(rest)).astype(BF16)
        kdec_pad = jnp.concatenate([kdec, jnp.zeros((n, C_KEY_DIM), BF16)], axis=0)
        v_t = jnp.concatenate([v_bf.astype(F32), jnp.zeros((n, C_VAL_DIM), F32)], axis=0).T.astype(BF16)
        s_ref[0, u] = st * jnp.exp(bcum[last:last + 1, :]) + _dot(v_t, kdec_pad)

    def body(h, carry):
        unit(0, h)
        unit(1, h)
        return carry

    lax.fori_loop(0, C_HEADS, body, 0)


def _gla_scan(q, k, v, low, w2, gate_b, mats, s0, nseq, lseq):
    nc = lseq // CHUNK
    low3 = low.reshape(nseq, lseq, LANES)
    mf, mb = mats

    def hspec(width, rev):
        return pl.BlockSpec((1, C_HEADS, CHUNK, width), (lambda b, c: (b, 0, nc - 1 - c, 0)) if rev else (lambda b, c: (b, 0, c, 0)))

    def lspec(rev):
        return pl.BlockSpec((1, CHUNK, LANES), (lambda b, c: (b, nc - 1 - c, 0)) if rev else (lambda b, c: (b, c, 0)))

    def cspec(shape):
        nd = len(shape)
        return pl.BlockSpec(shape, lambda b, c: (0,) * nd)

    sspec = pl.BlockSpec((1, 2 * C_HEADS, C_VAL_DIM, C_KEY_DIM), lambda b, c: (b, 0, 0, 0))
    oshape = jax.ShapeDtypeStruct((nseq, C_HEADS, lseq, C_VAL_DIM), F32)
    return pl.pallas_call(
        _gla_scan_kernel, grid=(nseq, nc),
        in_specs=[hspec(C_KEY_DIM, False), hspec(C_KEY_DIM, False), hspec(C_VAL_DIM, False), lspec(False),
                  hspec(C_KEY_DIM, True), hspec(C_KEY_DIM, True), hspec(C_VAL_DIM, True), lspec(True),
                  cspec(w2.shape), cspec(gate_b.shape), cspec(mf.shape), cspec(mb.shape), sspec],
        out_specs=[hspec(C_VAL_DIM, False), hspec(C_VAL_DIM, True), sspec],
        out_shape=[oshape, oshape, jax.ShapeDtypeStruct(s0.shape, F32)],
        scratch_shapes=[pltpu.VMEM((2 * C_HEADS, (N_LEVELS + 2) * CHUNK, C_KEY_DIM), F32)],
        compiler_params=_cparams(2, 32), name="gla_scan",
    )(q, k, v, low3, q, k, v, low3, w2, gate_b, mf, mb, s0)


def _out_proj_kernel(*refs, heads, head_dim):
    if heads:
        of_ref, ob_ref, gate_ref, nw_ref, w_ref, x_ref, mod_ref, g_ref, b_ref, o_ref = refs
        parts = []
        for h in range(heads):
            o = of_ref[0, h] + ob_ref[0, h]
            o = o * lax.rsqrt(jnp.mean(o * o, -1, keepdims=True) + NORM_EPS) * nw_ref[...]
            gate = gate_ref[:, h * head_dim:(h + 1) * head_dim].astype(F32)
            parts.append((o * _silu(gate)).astype(BF16))
        inp = jnp.concatenate(parts, axis=1)
    else:
        inp_ref, w_ref, x_ref, mod_ref, g_ref, b_ref, o_ref = refs
        inp = inp_ref[...]
    y = _dot(inp, w_ref[...])
    t = DN_ALPHA * x_ref[...] + mod_ref[0, 2:3, :] * y
    o_ref[...] = _layer_norm(t, g_ref[...], b_ref[...])


def _out_proj(st, w_bf, ln_g, ln_b, inp=None, scan_out=None, gate=None, gate_col=0, norm_w=None):
    common_specs = [_const_spec(w_bf.shape), st.row_spec(D_MODEL), st.mod_spec(),
                    _const_spec((1, D_MODEL)), _const_spec((1, D_MODEL))]
    common_args = [w_bf, st.x, st.mod, ln_g.reshape(1, -1), ln_b.reshape(1, -1)]
    if scan_out is None:
        heads = head_dim = 0
        in_specs = [st.row_spec(inp.shape[1])] + common_specs
        args = [inp] + common_args
    else:
        o_f, o_b = scan_out
        heads, head_dim = o_f.shape[1], o_f.shape[3]
        in_specs = [st.head_spec(heads, head_dim), st.head_spec(heads, head_dim),
                    st.row_spec(heads * head_dim, gate_col), _const_spec((1, head_dim))] + common_specs
        args = [o_f, o_b, gate, norm_w.reshape(1, -1)] + common_args
    return pl.pallas_call(
        functools.partial(_out_proj_kernel, heads=heads, head_dim=head_dim),
        grid=st.grid, in_specs=in_specs, out_specs=st.row_spec(D_MODEL),
        out_shape=jax.ShapeDtypeStruct((st.rows, D_MODEL), F32),
        compiler_params=_cparams(1, 40), name="out_proj",
    )(*args)


def _ffn_kernel(x_ref, xp_ref, xn_ref, mod_ref, wu_ref, cw_ref, cb_ref, wd_ref, g_ref, b_ref, o_ref,
                h_ref, act_ref, *, tps):
    t = pl.program_id(0) % tps
    tm = x_ref.shape[0]
    pad = SUBLANES
    x = x_ref[...]
    h_ref[pad:pad + tm, :] = _modulated(x, mod_ref, 3).astype(BF16)
    h_ref[0:pad, :] = (_modulated(xp_ref[...], mod_ref, 3) * (t > 0).astype(F32)).astype(BF16)
    h_ref[pad + tm:, :] = (_modulated(xn_ref[...], mod_ref, 3) * (t < tps - 1).astype(F32)).astype(BF16)
    for c0 in range(0, D_FF, FF_CHUNK):
        val = _dot(h_ref[pad:pad + tm, :], wu_ref[:, c0:c0 + FF_CHUNK])
        gx = _dot(h_ref[...], wu_ref[:, D_FF + c0:D_FF + c0 + FF_CHUNK])
        gt = (cw_ref[0:1, c0:c0 + FF_CHUNK] * pltpu.roll(gx, 1, 0)[pad:pad + tm]
              + cw_ref[1:2, c0:c0 + FF_CHUNK] * gx[pad:pad + tm]
              + cw_ref[2:3, c0:c0 + FF_CHUNK] * pltpu.roll(gx, tm + 2 * pad - 1, 0)[pad:pad + tm]
              + cb_ref[:, c0:c0 + FF_CHUNK])
        act_ref[:, c0:c0 + FF_CHUNK] = (_silu(gt) * val).astype(BF16)
    y = _dot(act_ref[...], wd_ref[...])
    o_ref[...] = _layer_norm(DN_ALPHA * x + mod_ref[0, 5:6, :] * y, g_ref[...], b_ref[...])


def _ffn(st, w_up, conv_w, conv_b, w_down, ln_g, ln_b):
    tm, tps = st.tm, st.tps
    r8 = tm // SUBLANES
    nb8 = st.rows // SUBLANES
    return pl.pallas_call(
        functools.partial(_ffn_kernel, tps=tps), grid=st.grid,
        in_specs=[st.row_spec(D_MODEL),
                  pl.BlockSpec((SUBLANES, D_MODEL), lambda i: (jnp.maximum(i * r8 - 1, 0), 0)),
                  pl.BlockSpec((SUBLANES, D_MODEL), lambda i: (jnp.minimum((i + 1) * r8, nb8 - 1), 0)),
                  st.mod_spec(), _const_spec(w_up.shape), _const_spec(conv_w.shape), _const_spec((1, D_FF)),
                  _const_spec(w_down.shape), _const_spec((1, D_MODEL)), _const_spec((1, D_MODEL))],
        out_specs=st.row_spec(D_MODEL),
        out_shape=jax.ShapeDtypeStruct((st.rows, D_MODEL), F32),
        scratch_shapes=[pltpu.VMEM((tm + 2 * SUBLANES, D_MODEL), BF16), pltpu.VMEM((tm, D_FF), BF16)],
        compiler_params=_cparams(1, 56), name="ffn",
    )(st.x, st.x, st.x, st.mod, w_up, conv_w, conv_b.reshape(1, -1), w_down, ln_g.reshape(1, -1), ln_b.reshape(1, -1))


def _rope_tables(n):
    rows = n // GRID_W
    r = jnp.repeat(jnp.arange(rows), GRID_W).astype(F32)
    col = jnp.tile(jnp.arange(GRID_W), rows).astype(F32)
    n_freq = A_HEAD_DIM // 4
    inv = jnp.power(ROPE_BASE, -jnp.arange(n_freq, dtype=F32) / n_freq)
    ang = jnp.concatenate([r[:, None] * inv, col[:, None] * inv], -1)
    cos, sin = jnp.cos(ang), jnp.sin(ang)
    zero = jnp.zeros_like(sin)
    reps = LANES // A_HEAD_DIM
    c = jnp.tile(jnp.concatenate([cos, cos], -1), (1, reps))
    s1 = jnp.tile(jnp.concatenate([-sin, zero], -1), (1, reps))
    s2 = jnp.tile(jnp.concatenate([zero, sin], -1), (1, reps))
    return c, s1, s2


def _pad_cols(w, width):
    return jnp.pad(w, ((0, 0), (0, width - w.shape[1])))


def kernel(x, c, ctx, c_ctx, ada_w, ada_b, ln_g, ln_b, ffn_w_up, ffn_conv_w, ffn_conv_b, ffn_w_down, attn_w_qkv, attn_sink, attn_w_o, gdn_w_in, gdn_conv_w, gdn_a_log, gdn_dt_bias, gdn_norm_w, gdn_w_o, gla_w_in, gla_w_gate2, gla_gate_b, gla_norm_w, gla_w_o):
    bsz, n, d = x.shape
    nctx = ctx.shape[1]
    assert d == D_MODEL and n % A_BLOCK == 0 and n % GRID_W == 0 and nctx % CHUNK == 0 and n % CHUNK == 0

    cond_rows = -(-(bsz + 1) // SUBLANES) * SUBLANES
    cond = jnp.zeros((cond_rows, d), F32).at[:bsz].set(c).at[bsz].set(c_ctx)
    mod_all = _ada(cond, ada_w, ada_b)

    rope_tabs = _rope_tables(n)
    gla_mats = tuple(jnp.asarray(m, BF16) for m in _gla_level_mats())

    xl = x.reshape(bsz * n, d)
    xc = ctx.reshape(bsz * nctx, d)
    for i in range(DEPTH):
        need_ctx_out = i < DEPTH - 1
        lat = _Stream(xl, mod_all[i, :bsz].reshape(bsz, 6, d), bsz, n)
        cx = _Stream(xc, mod_all[i, bsz:bsz + 1].reshape(1, 6, d), bsz, nctx)
        kind, slot = i % N_MIXERS, i // N_MIXERS
        if kind == 0:
            w_qkv = attn_w_qkv[slot].astype(BF16)
            w_o = attn_w_o[slot].astype(BF16)
            qkv_c = _attn_proj(cx, w_qkv, None)
            qkv_l = _attn_proj(lat, w_qkv, rope_tabs)
            o_l = _attention(qkv_l, qkv_c, attn_sink[slot], bsz, n, nctx, True)
            mix_l = dict(inp=o_l)
            if need_ctx_out:
                mix_c = dict(inp=_attention(qkv_c, qkv_c, attn_sink[slot], bsz, nctx, nctx, False))
        elif kind == 1:
            hk = B_HEADS * B_HEAD_DIM
            w_in = gdn_w_in[slot]
            w_main = w_in[:, :4 * hk].astype(BF16)
            w_small = _pad_cols(w_in[:, 4 * hk:], LANES).astype(BF16)
            w_o = gdn_w_o[slot].astype(BF16)
            conv_w3 = gdn_conv_w[slot].reshape(-1, 3 * B_HEADS, 1, LANES)
            a_vec = jnp.zeros((1, LANES), F32).at[0, 2 * B_HEADS:4 * B_HEADS].set(jnp.exp(gdn_a_log[slot]).reshape(-1))
            dt_vec = jnp.zeros((1, LANES), F32).at[0, 2 * B_HEADS:4 * B_HEADS].set(gdn_dt_bias[slot].reshape(-1))
            state = jnp.zeros((bsz, 2 * B_HEADS, B_HEAD_DIM, B_HEAD_DIM), F32)
            outs = []
            for st in (cx, lat):
                zh, gate, small = _gdn_proj(st, w_main, w_small)
                q, k, v, bg = _gdn_act(zh, small, conv_w3, a_vec, dt_vec, st.nseq, st.lseq)
                o_f, o_b, state = _gdn_scan(q, k, v, bg, state, st.nseq, st.lseq)
                outs.append(dict(scan_out=(o_f, o_b), gate=gate, norm_w=gdn_norm_w[slot]))
            mix_c, mix_l = outs
        else:
            dk, dv = C_HEADS * C_KEY_DIM, C_HEADS * C_VAL_DIM
            w_in = gla_w_in[slot]
            w_main = w_in[:, :2 * dk + 2 * dv].astype(BF16)
            w_small = _pad_cols(w_in[:, 2 * dk + 2 * dv:], LANES).astype(BF16)
            w_o = gla_w_o[slot].astype(BF16)
            w2 = jnp.zeros((LANES, 2 * dk), F32)
            for z in range(2):
                w2 = w2.at[z * C_GATE_RANK:(z + 1) * C_GATE_RANK, z * dk:(z + 1) * dk].set(gla_w_gate2[slot, z])
            gate_b = gla_gate_b[slot].reshape(1, 2 * dk)
            state = jnp.zeros((bsz, 2 * C_HEADS, C_VAL_DIM, C_KEY_DIM), F32)
            outs = []
            for st in (cx, lat):
                q, k, v, gate, low = _gla_proj(st, w_main, w_small)
                o_f, o_b, state = _gla_scan(q, k, v, low, w2, gate_b, gla_mats, state, st.nseq, st.lseq)
                outs.append(dict(scan_out=(o_f, o_b), gate=gate, norm_w=gla_norm_w[slot]))
            mix_c, mix_l = outs

        w_up = ffn_w_up[i].astype(BF16)
        w_down = ffn_w_down[i].astype(BF16)
        streams = [(lat, mix_l)] + ([(cx, mix_c)] if need_ctx_out else [])
        new = []
        for st, mix in streams:
            mid = _out_proj(st, w_o, ln_g[i, 0], ln_b[i, 0], **mix)
            st_mid = _Stream(mid, st.mod, st.nseq, st.lseq)
            new.append(_ffn(st_mid, w_up, ffn_conv_w[i], ffn_conv_b[i], w_down, ln_g[i, 1], ln_b[i, 1]))
        xl = new[0]
        if need_ctx_out:
            xc = new[1]
    return xl.reshape(bsz, n, d)
```

```python
import functools
import math

import numpy as np
import jax
import jax.numpy as jnp
from jax import lax
from jax.experimental import pallas as pl
from jax.experimental.pallas import tpu as pltpu

F32 = jnp.float32
BF16 = jnp.bfloat16

D_MODEL = 1024
DEPTH = 4
GRID_W = 64
N_MIXERS = 3
D_FF = 2816
A_HEADS = 16
A_KV_HEADS = 4
A_HEAD_DIM = 64
A_WINDOW = 128
A_BLOCK = 128
ROPE_BASE = 10000.0
B_HEADS = 8
B_HEAD_DIM = 128
C_HEADS = 4
C_KEY_DIM = 128
C_VAL_DIM = 256
C_GATE_RANK = 16
C_GATE_TAU = 16.0
CHUNK = 64
NORM_EPS = 1e-5
DN_ALPHA = (2 * DEPTH) ** 0.25

LANES = 128
SUBLANES = 8
V7X_VMEM_BYTES = 64 * 1024 * 1024
MIB = 1024 * 1024

ROW_TILE = 512
FF_CHUNK = 256
NEG_BIG = -1e30


def _cparams(n_axes, vmem_mib):
    return pltpu.CompilerParams(dimension_semantics=("arbitrary",) * n_axes,
                                vmem_limit_bytes=min(vmem_mib * MIB, V7X_VMEM_BYTES - 8 * MIB))


def _row_tile(lseq, target=ROW_TILE):
    t = min(lseq, target)
    while lseq % t:
        t -= SUBLANES
    return t


def _dot(a, b):
    return jnp.dot(a, b, preferred_element_type=F32)


def _dot_nt(a, b):
    return lax.dot_general(a, b, (((1,), (1,)), ((), ())), preferred_element_type=F32)


def _split3(x):
    hi = x.astype(BF16)
    r = x - hi.astype(F32)
    mid = r.astype(BF16)
    lo = (r - mid.astype(F32)).astype(BF16)
    return hi, mid, lo


def _dot_sel(m_bf, x):
    hi, mid, lo = _split3(x)
    return _dot(m_bf, hi) + _dot(m_bf, mid) + _dot(m_bf, lo)


def _silu(x):
    return x * jax.nn.sigmoid(x)


def _softplus(x):
    return jnp.maximum(x, 0.0) + jnp.log1p(jnp.exp(-jnp.abs(x)))


def _layer_norm(t, g, b):
    mu = jnp.mean(t, -1, keepdims=True)
    d = t - mu
    var = jnp.mean(d * d, -1, keepdims=True)
    return d * lax.rsqrt(var + NORM_EPS) * g + b


def _modulated(x, mod_ref, shift_row):
    sh = mod_ref[0, shift_row:shift_row + 1, :]
    sc = mod_ref[0, shift_row + 1:shift_row + 2, :]
    return x * (1.0 + sc) + sh


def _ada_kernel(s_ref, w_ref, b_ref, o_ref):
    s = _silu(s_ref[...])
    o_ref[0] = jnp.dot(s, w_ref[0], precision=lax.Precision.HIGHEST, preferred_element_type=F32) + b_ref[0]


def _ada(cond, ada_w, ada_b):
    depth, d, n = ada_w.shape
    rows = cond.shape[0]
    tn = 1536
    return pl.pallas_call(
        _ada_kernel,
        grid=(depth, n // tn),
        in_specs=[pl.BlockSpec((rows, d), lambda i, j: (0, 0)),
                  pl.BlockSpec((1, d, tn), lambda i, j: (i, 0, j)),
                  pl.BlockSpec((1, 1, tn), lambda i, j: (i, 0, j))],
        out_specs=pl.BlockSpec((1, rows, tn), lambda i, j: (i, 0, j)),
        out_shape=jax.ShapeDtypeStruct((depth, rows, n), F32),
        compiler_params=_cparams(2, 40),
        name="ada",
    )(cond, ada_w, ada_b.reshape(depth, 1, n))


class _Stream:
    def __init__(self, x, mod, nseq, lseq):
        self.x, self.mod, self.nseq, self.lseq = x, mod, nseq, lseq
        self.tm = _row_tile(lseq)
        self.tps = lseq // self.tm
        self.shared_mod = mod.shape[0] == 1

    def mod_spec(self):
        d = self.mod.shape[-1]
        if self.shared_mod:
            return pl.BlockSpec((1, 6, d), lambda i: (0, 0, 0))
        tps = self.tps
        return pl.BlockSpec((1, 6, d), lambda i: (i // tps, 0, 0))

    def row_spec(self, width, col_block=0):
        return pl.BlockSpec((self.tm, width), lambda i: (i, col_block))

    def head_spec(self, heads, width):
        tps = self.tps
        return pl.BlockSpec((1, heads, self.tm, width), lambda i: (i // tps, 0, i % tps, 0))

    def head_shape(self, heads, width, dtype):
        return jax.ShapeDtypeStruct((self.nseq, heads, self.lseq, width), dtype)

    @property
    def rows(self):
        return self.nseq * self.lseq

    @property
    def grid(self):
        return (self.rows // self.tm,)


def _const_spec(shape):
    nd = len(shape)
    return pl.BlockSpec(shape, lambda i: (0,) * nd)


def _attn_proj_kernel(x_ref, mod_ref, w_ref, *refs, rope):
    o_ref = refs[-1]
    u = _modulated(x_ref[...], mod_ref, 0).astype(BF16)
    z = _dot(u, w_ref[...])
    nq = A_HEADS * A_HEAD_DIM // LANES
    nk = A_KV_HEADS * A_HEAD_DIM // LANES
    for j in range(z.shape[1] // LANES):
        t = z[:, j * LANES:(j + 1) * LANES]
        if j < nq + nk and rope:
            c_ref, s1_ref, s2_ref = refs[:3]
            t = (t * c_ref[...] + pltpu.roll(t, LANES - A_HEAD_DIM // 2, 1) * s1_ref[...]
                 + pltpu.roll(t, A_HEAD_DIM // 2, 1) * s2_ref[...])
        if j < nq:
            t = t * (A_HEAD_DIM ** -0.5)
        o_ref[:, j * LANES:(j + 1) * LANES] = t.astype(BF16)


def _attn_proj(st, w_bf, rope_tabs):
    n = w_bf.shape[1]
    in_specs = [st.row_spec(D_MODEL), st.mod_spec(), _const_spec(w_bf.shape)]
    args = [st.x, st.mod, w_bf]
    if rope_tabs is not None:
        tps = st.tps
        for t in rope_tabs:
            in_specs.append(pl.BlockSpec((st.tm, LANES), lambda i: (i % tps, 0)))
            args.append(t)
    return pl.pallas_call(
        functools.partial(_attn_proj_kernel, rope=rope_tabs is not None),
        grid=st.grid, in_specs=in_specs, out_specs=st.row_spec(n),
        out_shape=jax.ShapeDtypeStruct((st.rows, n), BF16),
        compiler_params=_cparams(1, 40), name="attn_proj",
    )(*args)


def _gdn_proj_kernel(x_ref, mod_ref, w_ref, ws_ref, zh_ref, gate_ref, small_ref):
    u = _modulated(x_ref[...], mod_ref, 0).astype(BF16)
    z = _dot(u, w_ref[...])
    nh = zh_ref.shape[1]
    for j in range(nh):
        zh_ref[0, j] = z[:, j * LANES:(j + 1) * LANES].astype(BF16)
    gate_ref[...] = z[:, nh * LANES:].astype(BF16)
    small_ref[...] = _dot(u, ws_ref[...])


def _gdn_proj(st, w_main, w_small):
    hk = B_HEADS * B_HEAD_DIM
    nh = 3 * B_HEADS
    return pl.pallas_call(
        _gdn_proj_kernel, grid=st.grid,
        in_specs=[st.row_spec(D_MODEL), st.mod_spec(), _const_spec(w_main.shape), _const_spec(w_small.shape)],
        out_specs=[st.head_spec(nh, LANES), st.row_spec(hk), st.row_spec(LANES)],
        out_shape=[st.head_shape(nh, LANES, BF16), jax.ShapeDtypeStruct((st.rows, hk), BF16),
                   jax.ShapeDtypeStruct((st.rows, LANES), F32)],
        compiler_params=_cparams(1, 48), name="gdn_proj",
    )(st.x, st.mod, w_main, w_small)


def _gla_proj_kernel(x_ref, mod_ref, w_ref, ws_ref, q_ref, k_ref, v_ref, gate_ref, low_ref):
    u = _modulated(x_ref[...], mod_ref, 0).astype(BF16)
    z = _dot(u, w_ref[...])
    dk, dv = C_HEADS * C_KEY_DIM, C_HEADS * C_VAL_DIM
    for h in range(C_HEADS):
        q_ref[0, h] = (z[:, h * C_KEY_DIM:(h + 1) * C_KEY_DIM] * (C_KEY_DIM ** -0.5)).astype(BF16)
        k_ref[0, h] = z[:, dk + h * C_KEY_DIM:dk + (h + 1) * C_KEY_DIM].astype(BF16)
        v_ref[0, h] = z[:, 2 * dk + h * C_VAL_DIM:2 * dk + (h + 1) * C_VAL_DIM].astype(BF16)
    gate_ref[...] = z[:, 2 * dk + dv:].astype(BF16)
    low_ref[...] = _dot(u, ws_ref[...])


def _gla_proj(st, w_main, w_small):
    dv = C_HEADS * C_VAL_DIM
    return pl.pallas_call(
        _gla_proj_kernel, grid=st.grid,
        in_specs=[st.row_spec(D_MODEL), st.mod_spec(), _const_spec(w_main.shape), _const_spec(w_small.shape)],
        out_specs=[st.head_spec(C_HEADS, C_KEY_DIM), st.head_spec(C_HEADS, C_KEY_DIM),
                   st.head_spec(C_HEADS, C_VAL_DIM), st.row_spec(dv), st.row_spec(LANES)],
        out_shape=[st.head_shape(C_HEADS, C_KEY_DIM, BF16), st.head_shape(C_HEADS, C_KEY_DIM, BF16),
                   st.head_shape(C_HEADS, C_VAL_DIM, BF16), jax.ShapeDtypeStruct((st.rows, dv), BF16),
                   jax.ShapeDtypeStruct((st.rows, LANES), F32)],
        compiler_params=_cparams(1, 48), name="gla_proj",
    )(st.x, st.mod, w_main, w_small)


def _attn_kernel(sink_ref, q_ref, *refs, local, nblk):
    if local:
        kp_ref, kc_ref, kn_ref, vp_ref, vc_ref, vn_ref, kx_ref, vx_ref, o_ref = refs
    else:
        kx_ref, vx_ref, o_ref = refs
    j = pl.program_id(1)
    grp = A_HEADS // A_KV_HEADS
    hd = A_HEAD_DIM
    blk = q_ref.shape[0]
    q = q_ref[...]
    if local:
        span = 3 * blk
        shape = (grp * blk, span)
        qi = lax.broadcasted_iota(jnp.int32, shape, 0) & (blk - 1)
        r = lax.broadcasted_iota(jnp.int32, shape, 1)
        rel = r - qi
        lo = jnp.where(j > 0, 0, blk)
        hi = jnp.where(j < nblk - 1, span, 2 * blk)
        valid = (rel >= 0) & (rel <= 2 * A_WINDOW) & (r >= lo) & (r < hi)
        bias = jnp.where(valid, 0.0, NEG_BIG)
    heads = list(range(A_KV_HEADS))
    hs = [slice(h * hd, (h + 1) * hd) for h in heads]
    qh = [jnp.concatenate([q[:, (h * grp + g) * hd:(h * grp + g + 1) * hd] for g in range(grp)], axis=0) for h in heads]
    snk = [jnp.concatenate([jnp.full((blk, 1), sink_ref[h * grp + g], F32) for g in range(grp)], axis=0) for h in heads]
    s_ctx = [_dot_nt(a, kx_ref[:, s]) for a, s in zip(qh, hs)]
    m = _each(lambda s, k: jnp.maximum(jnp.max(s, -1, keepdims=True), k), s_ctx, snk)
    if local:
        kl = [jnp.concatenate([kp_ref[:, s], kc_ref[:, s], kn_ref[:, s]], axis=0) for s in hs]
        vl = [jnp.concatenate([vp_ref[:, s], vc_ref[:, s], vn_ref[:, s]], axis=0) for s in hs]
        s_loc = _each(lambda a, b: _dot_nt(a, b) + bias, qh, kl)
        m = _each(lambda a, s: jnp.maximum(a, jnp.max(s, -1, keepdims=True)), m, s_loc)
    p_ctx = _each(lambda s, a: jnp.exp(s - a), s_ctx, m)
    den = _each(lambda p, k, a: jnp.sum(p, -1, keepdims=True) + jnp.exp(k - a), p_ctx, snk, m)
    acc = [_dot(p.astype(BF16), vx_ref[:, s]) for p, s in zip(p_ctx, hs)]
    if local:
        p_loc = _each(lambda s, a: jnp.exp(s - a), s_loc, m)
        den = _each(lambda dn, p: dn + jnp.sum(p, -1, keepdims=True), den, p_loc)
        acc = _each(lambda a, p, v: a + _dot(p.astype(BF16), v), acc, p_loc, vl)
    for h, a, dn in zip(heads, acc, den):
        o = a / dn
        for g in range(grp):
            c0 = (h * grp + g) * hd
            o_ref[:, c0:c0 + hd] = o[g * blk:(g + 1) * blk].astype(BF16)


def _attention(qkv, qkv_ctx, sink, nseq, lseq, lctx, local):
    blk = A_BLOCK
    nblk = lseq // blk
    nq = A_HEADS * A_HEAD_DIM
    nkv = A_KV_HEADS * A_HEAD_DIM
    kcol, vcol = nq // nkv, nq // nkv + 1

    def row(f):
        return lambda b, j: (b * nblk + f(j), 0)

    in_specs = [pl.BlockSpec(memory_space=pltpu.SMEM), pl.BlockSpec((blk, nq), row(lambda j: j))]
    args = [sink, qkv]
    if local:
        for col in (kcol, vcol):
            for f in (lambda j: jnp.maximum(j - 1, 0), lambda j: j, lambda j: jnp.minimum(j + 1, nblk - 1)):
                in_specs.append(pl.BlockSpec((blk, nkv), (lambda f, col: lambda b, j: (b * nblk + f(j), col))(f, col)))
                args.append(qkv)
    for col in (kcol, vcol):
        in_specs.append(pl.BlockSpec((lctx, nkv), (lambda col: lambda b, j: (b, col))(col)))
        args.append(qkv_ctx)
    return pl.pallas_call(
        functools.partial(_attn_kernel, local=local, nblk=nblk),
        grid=(nseq, nblk), in_specs=in_specs,
        out_specs=pl.BlockSpec((blk, nq), row(lambda j: j)),
        out_shape=jax.ShapeDtypeStruct((nseq * lseq, nq), BF16),
        compiler_params=_cparams(2, 32), name="attn_local" if local else "attn_ctx",
    )(*args)


def _gdn_act_kernel(z_ref, zp_ref, zn_ref, small_ref, cw_ref, av_ref, dt_ref, q_ref, k_ref, v_ref, bg_ref, *, tps):
    t = pl.program_id(0) % tps
    has_prev = (t > 0).astype(F32)
    has_next = (t < tps - 1).astype(F32)
    tm = z_ref.shape[2]
    rows = lax.broadcasted_iota(jnp.int32, (tm, LANES), 0)
    first, last = rows == 0, rows == tm - 1
    for j in range(3 * B_HEADS):
        zc = z_ref[0, j].astype(F32)
        prev_row = zp_ref[0, j, SUBLANES - 1:SUBLANES, :].astype(F32) * has_prev
        next_row = zn_ref[0, j, 0:1, :].astype(F32) * has_next
        zp = jnp.where(first, prev_row, pltpu.roll(zc, 1, 0))
        zn = jnp.where(last, next_row, pltpu.roll(zc, tm - 1, 0))
        y = _silu(cw_ref[0, j] * zp + cw_ref[1, j] * zc + cw_ref[2, j] * zn)
        if j < 2 * B_HEADS:
            y = y * lax.rsqrt(jnp.sum(y * y, -1, keepdims=True) + 1e-6)
        if j < B_HEADS:
            q_ref[0, j] = (y * (B_HEAD_DIM ** -0.5)).astype(BF16)
        elif j < 2 * B_HEADS:
            k_ref[0, j - B_HEADS] = y.astype(BF16)
        else:
            v_ref[0, j - 2 * B_HEADS] = y.astype(BF16)
    zs = small_ref[...]
    lane = lax.broadcasted_iota(jnp.int32, zs.shape, 1)
    beta = jax.nn.sigmoid(zs)
    g = -av_ref[...] * _softplus(zs + dt_ref[...])
    bg_ref[...] = jnp.where(lane < 2 * B_HEADS, beta, g)


def _gdn_act(zh, small, conv_w3, a_vec, dt_vec, nseq, lseq):
    tm = _row_tile(lseq)
    tps = lseq // tm
    nh = 3 * B_HEADS
    nb8 = lseq // SUBLANES
    r8 = tm // SUBLANES

    def seq(i):
        return i // tps

    out_spec = pl.BlockSpec((1, B_HEADS, tm, LANES), lambda i: (seq(i), 0, i % tps, 0))
    out_shape = jax.ShapeDtypeStruct((nseq, B_HEADS, lseq, LANES), BF16)
    return pl.pallas_call(
        functools.partial(_gdn_act_kernel, tps=tps),
        grid=(nseq * tps,),
        in_specs=[pl.BlockSpec((1, nh, tm, LANES), lambda i: (seq(i), 0, i % tps, 0)),
                  pl.BlockSpec((1, nh, SUBLANES, LANES), lambda i: (seq(i), 0, jnp.maximum((i % tps) * r8 - 1, 0), 0)),
                  pl.BlockSpec((1, nh, SUBLANES, LANES),
                               lambda i: (seq(i), 0, jnp.minimum((i % tps + 1) * r8, nb8 - 1), 0)),
                  pl.BlockSpec((tm, LANES), lambda i: (i, 0)),
                  _const_spec(conv_w3.shape), _const_spec(a_vec.shape), _const_spec(dt_vec.shape)],
        out_specs=[out_spec, out_spec, out_spec, pl.BlockSpec((tm, LANES), lambda i: (i, 0))],
        out_shape=[out_shape, out_shape, out_shape, jax.ShapeDtypeStruct((nseq * lseq, LANES), F32)],
        compiler_params=_cparams(1, 48), name="gdn_act",
    )(zh, zh, zh, small, conv_w3, a_vec, dt_vec)


def _each(f, *lists):
    return [f(*a) for a in zip(*lists)]


def _unit_lower_inverse_minus_identity(lmats, ri, ci):
    b16 = (ri >> 4) == (ci >> 4)
    b32 = (ri >> 5) == (ci >> 5)

    def mm(a, b):
        return _dot(a.astype(BF16), b.astype(BF16))

    l_bd = _each(lambda l: jnp.where(b16, l, 0.0), lmats)
    m2 = _each(mm, l_bd, l_bd)
    m4 = _each(mm, m2, m2)
    p = _each(lambda l, a: a - l - mm(l, a), l_bd, m2)
    m8 = _each(mm, m4, m4)
    p = _each(lambda x, a: x + a + mm(x, a), p, m4)
    p = _each(lambda x, a: x + a + mm(x, a), p, m8)
    for sel in (lambda l: jnp.where(b32 & ~b16, l, 0.0), lambda l: jnp.where(b32, 0.0, l)):
        lo = _each(sel, lmats)
        y = _each(lambda x, a: a + mm(x, a), p, lo)
        p = _each(lambda x, a: x - (a + mm(a, x)), p, y)
    return p


def _gdn_scan_kernel(qf_ref, kf_ref, vf_ref, bgf_ref, qb_ref, kb_ref, vb_ref, bgb_ref, s0_ref,
                     of_ref, ob_ref, s_ref):
    c = pl.program_id(1)

    @pl.when(c == 0)
    def _():
        s_ref[...] = s0_ref[...]

    n = CHUNK
    ri = lax.broadcasted_iota(jnp.int32, (n, n), 0)
    ci = lax.broadcasted_iota(jnp.int32, (n, n), 1)
    zpad = jnp.zeros((n, LANES), F32)
    dirs = ((qf_ref, kf_ref, vf_ref, bgf_ref, of_ref, ri >= ci, ri > ci, n - 1),
            (qb_ref, kb_ref, vb_ref, bgb_ref, ob_ref, ri <= ci, ri < ci, 0))
    cums = []
    for d in range(2):
        bg = dirs[d][3][0]
        gc_all = _dot_sel(dirs[d][5].astype(BF16), bg)
        cums.append((bg, gc_all, jnp.concatenate([gc_all, zpad], axis=0).T))

    units = [(d, h) for h in range(B_HEADS) for d in range(2)]
    incl = [dirs[d][5] for d, _ in units]
    strict = [dirs[d][6] for d, _ in units]
    q = [dirs[d][0][0, h].astype(F32) for d, h in units]
    k = [dirs[d][1][0, h].astype(F32) for d, h in units]
    v = [dirs[d][2][0, h].astype(F32) for d, h in units]
    beta = [cums[d][0][:, d * B_HEADS + h:d * B_HEADS + h + 1] for d, h in units]
    gcol = [(2 + d) * B_HEADS + h for d, h in units]
    gc = [cums[d][1][:, g:g + 1] for (d, _), g in zip(units, gcol)]
    gc_row = [cums[d][2][g:g + 1, :n] for (d, _), g in zip(units, gcol)]
    gl = [a[dirs[d][7]:dirs[d][7] + 1, :] for (d, _), a in zip(units, gc)]
    decay = _each(lambda m, a, b: jnp.where(m, jnp.exp(jnp.where(m, a - b, 0.0)), 0.0), incl, gc, gc_row)
    kbeta = _each(lambda a, b: a * b, k, beta)
    k_bf = _each(lambda a: a.astype(BF16), k)
    lmat = _each(lambda m, a, b, dc: jnp.where(m, _dot_nt(a.astype(BF16), b) * dc, 0.0), strict, kbeta, k_bf, decay)
    a_qk = _each(lambda m, a, b, dc: jnp.where(m, _dot_nt(a.astype(BF16), b) * dc, 0.0), incl, q, k_bf, decay)
    tp = _unit_lower_inverse_minus_identity(lmat, ri, ci)
    egc = _each(jnp.exp, gc)
    rhs = _each(lambda a, b, kb, e: jnp.concatenate([a * b, kb * e], axis=1), v, beta, kbeta, egc)
    uw = _each(lambda r, t: r + _dot(t.astype(BF16), r.astype(BF16)), rhs, tp)
    kdec_t = _each(lambda a, t, c: jnp.concatenate([a * jnp.exp(t - c), zpad], axis=0).T.astype(BF16), k, gl, gc)
    qdec = _each(lambda a, e: (a * e).astype(BF16), q, egc)
    s = [s_ref[0, d * B_HEADS + h] for d, h in units]
    s_bf = _each(lambda a: a.astype(BF16), s)
    vn_bf = _each(lambda a, b: (a[:, :B_HEAD_DIM] - _dot(a[:, B_HEAD_DIM:].astype(BF16), b)).astype(BF16), uw, s_bf)
    o = _each(lambda a, b, m, w: _dot(a, b) + _dot(m.astype(BF16), w), qdec, s_bf, a_qk, vn_bf)
    bpad = jnp.zeros((n, LANES), BF16)
    s_new = _each(lambda a, t, kt, w: a * jnp.exp(t) + _dot(kt, jnp.concatenate([w, bpad], axis=0)), s, gl, kdec_t, vn_bf)
    for (d, h), o_u, s_u in zip(units, o, s_new):
        dirs[d][4][0, h] = o_u
        s_ref[0, d * B_HEADS + h] = s_u


def _gdn_scan(q, k, v, bg, s0, nseq, lseq):
    nc = lseq // CHUNK
    bg3 = bg.reshape(nseq, lseq, LANES)
    hspec_f = pl.BlockSpec((1, B_HEADS, CHUNK, LANES), lambda b, c: (b, 0, c, 0))
    hspec_b = pl.BlockSpec((1, B_HEADS, CHUNK, LANES), lambda b, c: (b, 0, nc - 1 - c, 0))
    gspec_f = pl.BlockSpec((1, CHUNK, LANES), lambda b, c: (b, c, 0))
    gspec_b = pl.BlockSpec((1, CHUNK, LANES), lambda b, c: (b, nc - 1 - c, 0))
    sspec = pl.BlockSpec((1, 2 * B_HEADS, B_HEAD_DIM, B_HEAD_DIM), lambda b, c: (b, 0, 0, 0))
    oshape = jax.ShapeDtypeStruct((nseq, B_HEADS, lseq, LANES), F32)
    return pl.pallas_call(
        _gdn_scan_kernel, grid=(nseq, nc),
        in_specs=[hspec_f, hspec_f, hspec_f, gspec_f, hspec_b, hspec_b, hspec_b, gspec_b, sspec],
        out_specs=[hspec_f, hspec_b, sspec],
        out_shape=[oshape, oshape, jax.ShapeDtypeStruct(s0.shape, F32)],
        compiler_params=_cparams(2, 32), name="gdn_scan",
    )(q, k, v, bg3, q, k, v, bg3, s0)


def _gla_level_mats():
    n = CHUNK
    mats = []
    s = n // 2
    while s >= 1:
        m = np.zeros((n, n), np.float32)
        for i in range(n):
            ref = (i // (2 * s)) * 2 * s + s
            if i >= ref:
                m[i, ref + 1:i + 1] = 1.0
            else:
                m[i, i + 1:ref + 1] = -1.0
        mats.append(m)
        s //= 2
    mats.append(np.tril(np.ones((n, n), np.float32)))
    mats.append(np.triu(np.ones((n, n), np.float32), 1))
    fwd = np.concatenate(mats, 0)
    bwd = np.concatenate([m[::-1, ::-1] for m in mats], 0)
    return fwd, bwd


N_LEVELS = int(math.log2(CHUNK))


def _gla_scan_kernel(qf_ref, kf_ref, vf_ref, lowf_ref, qb_ref, kb_ref, vb_ref, lowb_ref, w2_ref, gb_ref,
                     mf_ref, mb_ref, s0_ref, of_ref, ob_ref, s_ref):
    c = pl.program_id(1)

    @pl.when(c == 0)
    def _():
        s_ref[...] = s0_ref[...]

    n = CHUNK
    dk_all = C_HEADS * C_KEY_DIM
    ri = lax.broadcasted_iota(jnp.int32, (n, n), 0)
    ci = lax.broadcasted_iota(jnp.int32, (n, n), 1)
    dirs = ((qf_ref, kf_ref, vf_ref, lowf_ref, mf_ref, of_ref, n - 1), (qb_ref, kb_ref, vb_ref, lowb_ref, mb_ref, ob_ref, 0))

    levs = []
    for d in range(2):
        low_ref, m_ref = dirs[d][3], dirs[d][4]
        logit = jnp.dot(low_ref[0], w2_ref[:, d * dk_all:(d + 1) * dk_all], precision=lax.Precision.HIGHEST,
                        preferred_element_type=F32) + gb_ref[:, d * dk_all:(d + 1) * dk_all]
        log_a = (jnp.minimum(logit, 0.0) - jnp.log1p(jnp.exp(-jnp.abs(logit)))) * (1.0 / C_GATE_TAU)
        levs.append(_dot_sel(m_ref[...], log_a))

    units = [(d, h) for h in range(C_HEADS) for d in range(2)]

    def lev(blk):
        return [levs[d][blk * n:(blk + 1) * n, h * C_KEY_DIM:(h + 1) * C_KEY_DIM] for d, h in units]

    q = [dirs[d][0][0, h].astype(F32) for d, h in units]
    k = [dirs[d][1][0, h].astype(F32) for d, h in units]
    v_bf = [dirs[d][2][0, h] for d, h in units]
    a = _each(lambda x, y: jnp.where(ri == ci, _dot_nt(x.astype(BF16), y.astype(BF16)), 0.0), q, k)
    for lvl in range(N_LEVELS):
        sh = N_LEVELS - 1 - lvl
        same = (ri >> (sh + 1)) == (ci >> (sh + 1))
        hi_r, hi_c = ((ri >> sh) & 1) == 1, ((ci >> sh) & 1) == 1
        pair = (same & hi_r & ~hi_c, same & ~hi_r & hi_c)
        x = _each(lambda t: jnp.exp(-jnp.abs(t)), lev(lvl))
        a = [acc + jnp.where(pair[d], _dot_nt((qq * xx).astype(BF16), (kk * xx).astype(BF16)), 0.0)
             for (d, _), acc, qq, kk, xx in zip(units, a, q, k, x)]
    bcum = lev(N_LEVELS)
    rest = lev(N_LEVELS + 1)
    st = [s_ref[0, d * C_HEADS + h] for d, h in units]
    o = _each(lambda qq, b, s, aa, vv: _dot_nt((qq * jnp.exp(b)).astype(BF16), s.astype(BF16)) + _dot(aa.astype(BF16), vv),
              q, bcum, st, a, v_bf)
    kpad = jnp.zeros((n, C_KEY_DIM), BF16)
    vpad = jnp.zeros((n, C_VAL_DIM), F32)
    kdec = _each(lambda kk, r: jnp.concatenate([(kk * jnp.exp(r)).astype(BF16), kpad], axis=0), k, rest)
    v_t = _each(lambda vv: jnp.concatenate([vv.astype(F32), vpad], axis=0).T.astype(BF16), v_bf)
    s_new = [s * jnp.exp(b[dirs[d][6]:dirs[d][6] + 1, :]) + _dot(vt, kd)
             for (d, _), s, b, vt, kd in zip(units, st, bcum, v_t, kdec)]
    for (d, h), o_u, s_u in zip(units, o, s_new):
        dirs[d][5][0, h] = o_u
        s_ref[0, d * C_HEADS + h] = s_u


def _gla_scan(q, k, v, low, w2, gate_b, mats, s0, nseq, lseq):
    nc = lseq // CHUNK
    low3 = low.reshape(nseq, lseq, LANES)
    mf, mb = mats

    def hspec(width, rev):
        return pl.BlockSpec((1, C_HEADS, CHUNK, width), (lambda b, c: (b, 0, nc - 1 - c, 0)) if rev else (lambda b, c: (b, 0, c, 0)))

    def lspec(rev):
        return pl.BlockSpec((1, CHUNK, LANES), (lambda b, c: (b, nc - 1 - c, 0)) if rev else (lambda b, c: (b, c, 0)))

    def cspec(shape):
        nd = len(shape)
        return pl.BlockSpec(shape, lambda b, c: (0,) * nd)

    sspec = pl.BlockSpec((1, 2 * C_HEADS, C_VAL_DIM, C_KEY_DIM), lambda b, c: (b, 0, 0, 0))
    oshape = jax.ShapeDtypeStruct((nseq, C_HEADS, lseq, C_VAL_DIM), F32)
    return pl.pallas_call(
        _gla_scan_kernel, grid=(nseq, nc),
        in_specs=[hspec(C_KEY_DIM, False), hspec(C_KEY_DIM, False), hspec(C_VAL_DIM, False), lspec(False),
                  hspec(C_KEY_DIM, True), hspec(C_KEY_DIM, True), hspec(C_VAL_DIM, True), lspec(True),
                  cspec(w2.shape), cspec(gate_b.shape), cspec(mf.shape), cspec(mb.shape), sspec],
        out_specs=[hspec(C_VAL_DIM, False), hspec(C_VAL_DIM, True), sspec],
        out_shape=[oshape, oshape, jax.ShapeDtypeStruct(s0.shape, F32)],
        compiler_params=_cparams(2, 32), name="gla_scan",
    )(q, k, v, low3, q, k, v, low3, w2, gate_b, mf, mb, s0)


def _out_proj_kernel(*refs, heads, head_dim):
    if heads:
        of_ref, ob_ref, gate_ref, nw_ref, w_ref, x_ref, mod_ref, g_ref, b_ref, o_ref = refs
        parts = []
        for h in range(heads):
            o = of_ref[0, h] + ob_ref[0, h]
            o = o * lax.rsqrt(jnp.mean(o * o, -1, keepdims=True) + NORM_EPS) * nw_ref[...]
            gate = gate_ref[:, h * head_dim:(h + 1) * head_dim].astype(F32)
            parts.append((o * _silu(gate)).astype(BF16))
        inp = jnp.concatenate(parts, axis=1)
    else:
        inp_ref, w_ref, x_ref, mod_ref, g_ref, b_ref, o_ref = refs
        inp = inp_ref[...]
    y = _dot(inp, w_ref[...])
    t = DN_ALPHA * x_ref[...] + mod_ref[0, 2:3, :] * y
    o_ref[...] = _layer_norm(t, g_ref[...], b_ref[...])


def _out_proj(st, w_bf, ln_g, ln_b, inp=None, scan_out=None, gate=None, gate_col=0, norm_w=None):
    common_specs = [_const_spec(w_bf.shape), st.row_spec(D_MODEL), st.mod_spec(),
                    _const_spec((1, D_MODEL)), _const_spec((1, D_MODEL))]
    common_args = [w_bf, st.x, st.mod, ln_g.reshape(1, -1), ln_b.reshape(1, -1)]
    if scan_out is None:
        heads = head_dim = 0
        in_specs = [st.row_spec(inp.shape[1])] + common_specs
        args = [inp] + common_args
    else:
        o_f, o_b = scan_out
        heads, head_dim = o_f.shape[1], o_f.shape[3]
        in_specs = [st.head_spec(heads, head_dim), st.head_spec(heads, head_dim),
                    st.row_spec(heads * head_dim, gate_col), _const_spec((1, head_dim))] + common_specs
        args = [o_f, o_b, gate, norm_w.reshape(1, -1)] + common_args
    return pl.pallas_call(
        functools.partial(_out_proj_kernel, heads=heads, head_dim=head_dim),
        grid=st.grid, in_specs=in_specs, out_specs=st.row_spec(D_MODEL),
        out_shape=jax.ShapeDtypeStruct((st.rows, D_MODEL), F32),
        compiler_params=_cparams(1, 40), name="out_proj",
    )(*args)


def _ffn_kernel(x_ref, xp_ref, xn_ref, mod_ref, wu_ref, cw_ref, cb_ref, wd_ref, g_ref, b_ref, o_ref,
                h_ref, act_ref, *, tps):
    t = pl.program_id(0) % tps
    tm = x_ref.shape[0]
    pad = SUBLANES
    x = x_ref[...]
    h_ref[pad:pad + tm, :] = _modulated(x, mod_ref, 3).astype(BF16)
    h_ref[0:pad, :] = (_modulated(xp_ref[...], mod_ref, 3) * (t > 0).astype(F32)).astype(BF16)
    h_ref[pad + tm:, :] = (_modulated(xn_ref[...], mod_ref, 3) * (t < tps - 1).astype(F32)).astype(BF16)
    for c0 in range(0, D_FF, FF_CHUNK):
        val = _dot(h_ref[pad:pad + tm, :], wu_ref[:, c0:c0 + FF_CHUNK])
        gx = _dot(h_ref[...], wu_ref[:, D_FF + c0:D_FF + c0 + FF_CHUNK])
        gt = (cw_ref[0:1, c0:c0 + FF_CHUNK] * pltpu.roll(gx, 1, 0)[pad:pad + tm]
              + cw_ref[1:2, c0:c0 + FF_CHUNK] * gx[pad:pad + tm]
              + cw_ref[2:3, c0:c0 + FF_CHUNK] * pltpu.roll(gx, tm + 2 * pad - 1, 0)[pad:pad + tm]
              + cb_ref[:, c0:c0 + FF_CHUNK])
        act_ref[:, c0:c0 + FF_CHUNK] = (_silu(gt) * val).astype(BF16)
    y = _dot(act_ref[...], wd_ref[...])
    o_ref[...] = _layer_norm(DN_ALPHA * x + mod_ref[0, 5:6, :] * y, g_ref[...], b_ref[...])


def _ffn(st, w_up, conv_w, conv_b, w_down, ln_g, ln_b):
    tm, tps = st.tm, st.tps
    r8 = tm // SUBLANES
    nb8 = st.rows // SUBLANES
    return pl.pallas_call(
        functools.partial(_ffn_kernel, tps=tps), grid=st.grid,
        in_specs=[st.row_spec(D_MODEL),
                  pl.BlockSpec((SUBLANES, D_MODEL), lambda i: (jnp.maximum(i * r8 - 1, 0), 0)),
                  pl.BlockSpec((SUBLANES, D_MODEL), lambda i: (jnp.minimum((i + 1) * r8, nb8 - 1), 0)),
                  st.mod_spec(), _const_spec(w_up.shape), _const_spec(conv_w.shape), _const_spec((1, D_FF)),
                  _const_spec(w_down.shape), _const_spec((1, D_MODEL)), _const_spec((1, D_MODEL))],
        out_specs=st.row_spec(D_MODEL),
        out_shape=jax.ShapeDtypeStruct((st.rows, D_MODEL), F32),
        scratch_shapes=[pltpu.VMEM((tm + 2 * SUBLANES, D_MODEL), BF16), pltpu.VMEM((tm, D_FF), BF16)],
        compiler_params=_cparams(1, 56), name="ffn",
    )(st.x, st.x, st.x, st.mod, w_up, conv_w, conv_b.reshape(1, -1), w_down, ln_g.reshape(1, -1), ln_b.reshape(1, -1))


def _rope_tables(n):
    rows = n // GRID_W
    r = jnp.repeat(jnp.arange(rows), GRID_W).astype(F32)
    col = jnp.tile(jnp.arange(GRID_W), rows).astype(F32)
    n_freq = A_HEAD_DIM // 4
    inv = jnp.power(ROPE_BASE, -jnp.arange(n_freq, dtype=F32) / n_freq)
    ang = jnp.concatenate([r[:, None] * inv, col[:, None] * inv], -1)
    cos, sin = jnp.cos(ang), jnp.sin(ang)
    zero = jnp.zeros_like(sin)
    reps = LANES // A_HEAD_DIM
    c = jnp.tile(jnp.concatenate([cos, cos], -1), (1, reps))
    s1 = jnp.tile(jnp.concatenate([-sin, zero], -1), (1, reps))
    s2 = jnp.tile(jnp.concatenate([zero, sin], -1), (1, reps))
    return c, s1, s2


def _pad_cols(w, width):
    return jnp.pad(w, ((0, 0), (0, width - w.shape[1])))


def kernel(x, c, ctx, c_ctx, ada_w, ada_b, ln_g, ln_b, ffn_w_up, ffn_conv_w, ffn_conv_b, ffn_w_down, attn_w_qkv, attn_sink, attn_w_o, gdn_w_in, gdn_conv_w, gdn_a_log, gdn_dt_bias, gdn_norm_w, gdn_w_o, gla_w_in, gla_w_gate2, gla_gate_b, gla_norm_w, gla_w_o):
    bsz, n, d = x.shape
    nctx = ctx.shape[1]
    assert d == D_MODEL and n % A_BLOCK == 0 and n % GRID_W == 0 and nctx % CHUNK == 0 and n % CHUNK == 0

    cond_rows = -(-(bsz + 1) // SUBLANES) * SUBLANES
    cond = jnp.zeros((cond_rows, d), F32).at[:bsz].set(c).at[bsz].set(c_ctx)
    mod_all = _ada(cond, ada_w, ada_b)

    rope_tabs = _rope_tables(n)
    gla_mats = tuple(jnp.asarray(m, BF16) for m in _gla_level_mats())

    xl = x.reshape(bsz * n, d)
    xc = ctx.reshape(bsz * nctx, d)
    for i in range(DEPTH):
        need_ctx_out = i < DEPTH - 1
        lat = _Stream(xl, mod_all[i, :bsz].reshape(bsz, 6, d), bsz, n)
        cx = _Stream(xc, mod_all[i, bsz:bsz + 1].reshape(1, 6, d), bsz, nctx)
        kind, slot = i % N_MIXERS, i // N_MIXERS
        if kind == 0:
            w_qkv = attn_w_qkv[slot].astype(BF16)
            w_o = attn_w_o[slot].astype(BF16)
            qkv_c = _attn_proj(cx, w_qkv, None)
            qkv_l = _attn_proj(lat, w_qkv, rope_tabs)
            o_l = _attention(qkv_l, qkv_c, attn_sink[slot], bsz, n, nctx, True)
            mix_l = dict(inp=o_l)
            if need_ctx_out:
                mix_c = dict(inp=_attention(qkv_c, qkv_c, attn_sink[slot], bsz, nctx, nctx, False))
        elif kind == 1:
            hk = B_HEADS * B_HEAD_DIM
            w_in = gdn_w_in[slot]
            w_main = w_in[:, :4 * hk].astype(BF16)
            w_small = _pad_cols(w_in[:, 4 * hk:], LANES).astype(BF16)
            w_o = gdn_w_o[slot].astype(BF16)
            conv_w3 = gdn_conv_w[slot].reshape(-1, 3 * B_HEADS, 1, LANES)
            a_vec = jnp.zeros((1, LANES), F32).at[0, 2 * B_HEADS:4 * B_HEADS].set(jnp.exp(gdn_a_log[slot]).reshape(-1))
            dt_vec = jnp.zeros((1, LANES), F32).at[0, 2 * B_HEADS:4 * B_HEADS].set(gdn_dt_bias[slot].reshape(-1))
            state = jnp.zeros((bsz, 2 * B_HEADS, B_HEAD_DIM, B_HEAD_DIM), F32)
            outs = []
            for st in (cx, lat):
                zh, gate, small = _gdn_proj(st, w_main, w_small)
                q, k, v, bg = _gdn_act(zh, small, conv_w3, a_vec, dt_vec, st.nseq, st.lseq)
                o_f, o_b, state = _gdn_scan(q, k, v, bg, state, st.nseq, st.lseq)
                outs.append(dict(scan_out=(o_f, o_b), gate=gate, norm_w=gdn_norm_w[slot]))
            mix_c, mix_l = outs
        else:
            dk, dv = C_HEADS * C_KEY_DIM, C_HEADS * C_VAL_DIM
            w_in = gla_w_in[slot]
            w_main = w_in[:, :2 * dk + 2 * dv].astype(BF16)
            w_small = _pad_cols(w_in[:, 2 * dk + 2 * dv:], LANES).astype(BF16)
            w_o = gla_w_o[slot].astype(BF16)
            w2 = jnp.zeros((LANES, 2 * dk), F32)
            for z in range(2):
                w2 = w2.at[z * C_GATE_RANK:(z + 1) * C_GATE_RANK, z * dk:(z + 1) * dk].set(gla_w_gate2[slot, z])
            gate_b = gla_gate_b[slot].reshape(1, 2 * dk)
            state = jnp.zeros((bsz, 2 * C_HEADS, C_VAL_DIM, C_KEY_DIM), F32)
            outs = []
            for st in (cx, lat):
                q, k, v, gate, low = _gla_proj(st, w_main, w_small)
                o_f, o_b, state = _gla_scan(q, k, v, low, w2, gate_b, gla_mats, state, st.nseq, st.lseq)
                outs.append(dict(scan_out=(o_f, o_b), gate=gate, norm_w=gla_norm_w[slot]))
            mix_c, mix_l = outs

        w_up = ffn_w_up[i].astype(BF16)
        w_down = ffn_w_down[i].astype(BF16)
        streams = [(lat, mix_l)] + ([(cx, mix_c)] if need_ctx_out else [])
        new = []
        for st, mix in streams:
            mid = _out_proj(st, w_o, ln_g[i, 0], ln_b[i, 0], **mix)
            st_mid = _Stream(mid, st.mod, st.nseq, st.lseq)
            new.append(_ffn(st_mid, w_up, ffn_conv_w[i], ffn_conv_b[i], w_down, ln_g[i, 1], ln_b[i, 1]))
        xl = new[0]
        if need_ctx_out:
            xc = new[1]
    return xl.reshape(bsz, n, d)
```

```python
import functools
import math

import numpy as np
import jax
import jax.numpy as jnp
from jax import lax
from jax.experimental import pallas as pl
from jax.experimental.pallas import tpu as pltpu

F32 = jnp.float32
BF16 = jnp.bfloat16

D_MODEL = 1024
DEPTH = 4
GRID_W = 64
N_MIXERS = 3
D_FF = 2816
A_HEADS = 16
A_KV_HEADS = 4
A_HEAD_DIM = 64
A_WINDOW = 128
A_BLOCK = 128
ROPE_BASE = 10000.0
B_HEADS = 8
B_HEAD_DIM = 128
C_HEADS = 4
C_KEY_DIM = 128
C_VAL_DIM = 256
C_GATE_RANK = 16
C_GATE_TAU = 16.0
CHUNK = 64
NORM_EPS = 1e-5
DN_ALPHA = (2 * DEPTH) ** 0.25

LANES = 128
SUBLANES = 8
V7X_VMEM_BYTES = 64 * 1024 * 1024
MIB = 1024 * 1024

ROW_TILE = 512
FF_CHUNK = 256
NEG_BIG = -1e30


def _cparams(n_axes, vmem_mib):
    return pltpu.CompilerParams(dimension_semantics=("arbitrary",) * n_axes,
                                vmem_limit_bytes=min(vmem_mib * MIB, V7X_VMEM_BYTES - 8 * MIB))


def _row_tile(lseq, target=ROW_TILE):
    t = min(lseq, target)
    while lseq % t:
        t -= SUBLANES
    return t


def _dot(a, b):
    return jnp.dot(a, b, preferred_element_type=F32)


def _dot_nt(a, b):
    return lax.dot_general(a, b, (((1,), (1,)), ((), ())), preferred_element_type=F32)


def _split3(x):
    hi = x.astype(BF16)
    r = x - hi.astype(F32)
    mid = r.astype(BF16)
    lo = (r - mid.astype(F32)).astype(BF16)
    return hi, mid, lo


def _dot_sel(m_bf, x, parts=3):
    pieces = _split3(x)[:parts]
    acc = _dot(m_bf, pieces[0])
    for piece in pieces[1:]:
        acc = acc + _dot(m_bf, piece)
    return acc


def _silu(x):
    return x * jax.nn.sigmoid(x)


def _softplus(x):
    return jnp.maximum(x, 0.0) + jnp.log1p(jnp.exp(-jnp.abs(x)))


def _layer_norm(t, g, b):
    mu = jnp.mean(t, -1, keepdims=True)
    d = t - mu
    var = jnp.mean(d * d, -1, keepdims=True)
    return d * lax.rsqrt(var + NORM_EPS) * g + b


def _modulated(x, mod_ref, shift_row):
    sh = mod_ref[0, shift_row:shift_row + 1, :]
    sc = mod_ref[0, shift_row + 1:shift_row + 2, :]
    return x * (1.0 + sc) + sh


def _ada_kernel(s_ref, w_ref, b_ref, o_ref):
    s = _silu(s_ref[...])
    o_ref[0] = jnp.dot(s, w_ref[0], precision=lax.Precision.HIGHEST, preferred_element_type=F32) + b_ref[0]


def _ada(cond, ada_w, ada_b):
    depth, d, n = ada_w.shape
    rows = cond.shape[0]
    tn = 1536
    return pl.pallas_call(
        _ada_kernel,
        grid=(depth, n // tn),
        in_specs=[pl.BlockSpec((rows, d), lambda i, j: (0, 0)),
                  pl.BlockSpec((1, d, tn), lambda i, j: (i, 0, j)),
                  pl.BlockSpec((1, 1, tn), lambda i, j: (i, 0, j))],
        out_specs=pl.BlockSpec((1, rows, tn), lambda i, j: (i, 0, j)),
        out_shape=jax.ShapeDtypeStruct((depth, rows, n), F32),
        compiler_params=_cparams(2, 40),
        name="ada",
    )(cond, ada_w, ada_b.reshape(depth, 1, n))


class _Stream:
    def __init__(self, x, mod, nseq, lseq):
        self.x, self.mod, self.nseq, self.lseq = x, mod, nseq, lseq
        self.tm = _row_tile(lseq)
        self.tps = lseq // self.tm
        self.shared_mod = mod.shape[0] == 1

    def mod_spec(self):
        d = self.mod.shape[-1]
        if self.shared_mod:
            return pl.BlockSpec((1, 6, d), lambda i: (0, 0, 0))
        tps = self.tps
        return pl.BlockSpec((1, 6, d), lambda i: (i // tps, 0, 0))

    def row_spec(self, width, col_block=0):
        return pl.BlockSpec((self.tm, width), lambda i: (i, col_block))

    def head_spec(self, heads, width):
        tps = self.tps
        return pl.BlockSpec((1, heads, self.tm, width), lambda i: (i // tps, 0, i % tps, 0))

    def head_shape(self, heads, width, dtype):
        return jax.ShapeDtypeStruct((self.nseq, heads, self.lseq, width), dtype)

    @property
    def rows(self):
        return self.nseq * self.lseq

    @property
    def grid(self):
        return (self.rows // self.tm,)


def _const_spec(shape):
    nd = len(shape)
    return pl.BlockSpec(shape, lambda i: (0,) * nd)


def _attn_proj_kernel(x_ref, mod_ref, w_ref, *refs, rope):
    qt_ref, k_ref, vt_ref = refs[-3:]
    u = _modulated(x_ref[...], mod_ref, 0).astype(BF16)
    z = _dot(u, w_ref[...])
    nq = A_HEADS * A_HEAD_DIM // LANES
    nk = A_KV_HEADS * A_HEAD_DIM // LANES
    for j in range(z.shape[1] // LANES):
        t = z[:, j * LANES:(j + 1) * LANES]
        if j < nq + nk and rope:
            c_ref, s1_ref, s2_ref = refs[:3]
            t = (t * c_ref[...] + pltpu.roll(t, LANES - A_HEAD_DIM // 2, 1) * s1_ref[...]
                 + pltpu.roll(t, A_HEAD_DIM // 2, 1) * s2_ref[...])
        if j < nq:
            qt_ref[0, j * LANES:(j + 1) * LANES, :] = (t * (A_HEAD_DIM ** -0.5)).T.astype(BF16)
        elif j < nq + nk:
            k_ref[:, (j - nq) * LANES:(j - nq + 1) * LANES] = t.astype(BF16)
        else:
            vt_ref[0, (j - nq - nk) * LANES:(j - nq - nk + 1) * LANES, :] = t.T.astype(BF16)


def _attn_proj(st, w_bf, rope_tabs):
    nq = A_HEADS * A_HEAD_DIM
    nkv = A_KV_HEADS * A_HEAD_DIM
    tps = st.tps
    in_specs = [st.row_spec(D_MODEL), st.mod_spec(), _const_spec(w_bf.shape)]
    args = [st.x, st.mod, w_bf]
    if rope_tabs is not None:
        for t in rope_tabs:
            in_specs.append(pl.BlockSpec((st.tm, LANES), lambda i: (i % tps, 0)))
            args.append(t)

    def fm_spec(feat):
        return pl.BlockSpec((1, feat, st.tm), lambda i: (i // tps, 0, i % tps))

    return pl.pallas_call(
        functools.partial(_attn_proj_kernel, rope=rope_tabs is not None),
        grid=st.grid, in_specs=in_specs, out_specs=[fm_spec(nq), st.row_spec(nkv), fm_spec(nkv)],
        out_shape=[jax.ShapeDtypeStruct((st.nseq, nq, st.lseq), BF16), jax.ShapeDtypeStruct((st.rows, nkv), BF16),
                   jax.ShapeDtypeStruct((st.nseq, nkv, st.lseq), BF16)],
        compiler_params=_cparams(1, 40), name="attn_proj",
    )(*args)


def _gdn_proj_kernel(x_ref, xp_ref, xn_ref, mod_ref, w_ref, ws_ref, cw_ref, av_ref, dt_ref,
                     q_ref, k_ref, v_ref, gate_ref, bg_ref, u_ref, *, tps):
    t = pl.program_id(0) % tps
    tm = x_ref.shape[0]
    pad = SUBLANES
    hk = B_HEADS * B_HEAD_DIM
    u_ref[pad:pad + tm, :] = _modulated(x_ref[...], mod_ref, 0).astype(BF16)
    u_ref[0:pad, :] = (_modulated(xp_ref[...], mod_ref, 0) * (t > 0).astype(F32)).astype(BF16)
    u_ref[pad + tm:, :] = (_modulated(xn_ref[...], mod_ref, 0) * (t < tps - 1).astype(F32)).astype(BF16)
    width = 2 * LANES
    for c0 in range(0, 3 * hk, width):
        zx = _dot(u_ref[...], w_ref[:, c0:c0 + width])
        y = _silu(cw_ref[0:1, c0:c0 + width] * pltpu.roll(zx, 1, 0)[pad:pad + tm]
                  + cw_ref[1:2, c0:c0 + width] * zx[pad:pad + tm]
                  + cw_ref[2:3, c0:c0 + width] * pltpu.roll(zx, tm + 2 * pad - 1, 0)[pad:pad + tm])
        for j in range(c0 // LANES, (c0 + width) // LANES):
            yj = y[:, j * LANES - c0:(j + 1) * LANES - c0]
            if j < 2 * B_HEADS:
                yj = yj * lax.rsqrt(jnp.sum(yj * yj, -1, keepdims=True) + 1e-6)
            if j < B_HEADS:
                q_ref[0, j] = (yj * (B_HEAD_DIM ** -0.5)).astype(BF16)
            elif j < 2 * B_HEADS:
                k_ref[0, j - B_HEADS] = yj.astype(BF16)
            else:
                v_ref[0, j - 2 * B_HEADS] = yj.astype(BF16)
    u = u_ref[pad:pad + tm, :]
    gate_ref[...] = _dot(u, w_ref[:, 3 * hk:]).astype(BF16)
    zs = _dot(u, ws_ref[...])
    lane = lax.broadcasted_iota(jnp.int32, zs.shape, 1)
    bg_ref[...] = jnp.where(lane < 2 * B_HEADS, jax.nn.sigmoid(zs), -av_ref[...] * _softplus(zs + dt_ref[...]))


def _gdn_proj(st, w_main, w_small, conv_w, a_vec, dt_vec):
    hk = B_HEADS * B_HEAD_DIM
    tm, tps = st.tm, st.tps
    r8 = tm // SUBLANES
    nb8 = st.rows // SUBLANES
    hspec = st.head_spec(B_HEADS, LANES)
    hshape = st.head_shape(B_HEADS, LANES, BF16)
    return pl.pallas_call(
        functools.partial(_gdn_proj_kernel, tps=tps), grid=st.grid,
        in_specs=[st.row_spec(D_MODEL),
                  pl.BlockSpec((SUBLANES, D_MODEL), lambda i: (jnp.maximum(i * r8 - 1, 0), 0)),
                  pl.BlockSpec((SUBLANES, D_MODEL), lambda i: (jnp.minimum((i + 1) * r8, nb8 - 1), 0)),
                  st.mod_spec(), _const_spec(w_main.shape), _const_spec(w_small.shape), _const_spec(conv_w.shape),
                  _const_spec(a_vec.shape), _const_spec(dt_vec.shape)],
        out_specs=[hspec, hspec, hspec, st.row_spec(hk), st.row_spec(LANES)],
        out_shape=[hshape, hshape, hshape, jax.ShapeDtypeStruct((st.rows, hk), BF16),
                   jax.ShapeDtypeStruct((st.rows, LANES), F32)],
        scratch_shapes=[pltpu.VMEM((tm + 2 * SUBLANES, D_MODEL), BF16)],
        compiler_params=_cparams(1, 48), name="gdn_proj",
    )(st.x, st.x, st.x, st.mod, w_main, w_small, conv_w, a_vec, dt_vec)


def _gla_proj_kernel(x_ref, mod_ref, w_ref, ws_ref, q_ref, k_ref, v_ref, gate_ref, low_ref):
    u = _modulated(x_ref[...], mod_ref, 0).astype(BF16)
    z = _dot(u, w_ref[...])
    dk, dv = C_HEADS * C_KEY_DIM, C_HEADS * C_VAL_DIM
    for h in range(C_HEADS):
        q_ref[0, h] = (z[:, h * C_KEY_DIM:(h + 1) * C_KEY_DIM] * (C_KEY_DIM ** -0.5)).astype(BF16)
        k_ref[0, h] = z[:, dk + h * C_KEY_DIM:dk + (h + 1) * C_KEY_DIM].astype(BF16)
        v_ref[0, h] = z[:, 2 * dk + h * C_VAL_DIM:2 * dk + (h + 1) * C_VAL_DIM].astype(BF16)
    gate_ref[...] = z[:, 2 * dk + dv:].astype(BF16)
    low_ref[...] = _dot(u, ws_ref[...])


def _gla_proj(st, w_main, w_small):
    dv = C_HEADS * C_VAL_DIM
    return pl.pallas_call(
        _gla_proj_kernel, grid=st.grid,
        in_specs=[st.row_spec(D_MODEL), st.mod_spec(), _const_spec(w_main.shape), _const_spec(w_small.shape)],
        out_specs=[st.head_spec(C_HEADS, C_KEY_DIM), st.head_spec(C_HEADS, C_KEY_DIM),
                   st.head_spec(C_HEADS, C_VAL_DIM), st.row_spec(dv), st.row_spec(LANES)],
        out_shape=[st.head_shape(C_HEADS, C_KEY_DIM, BF16), st.head_shape(C_HEADS, C_KEY_DIM, BF16),
                   st.head_shape(C_HEADS, C_VAL_DIM, BF16), jax.ShapeDtypeStruct((st.rows, dv), BF16),
                   jax.ShapeDtypeStruct((st.rows, LANES), F32)],
        compiler_params=_cparams(1, 48), name="gla_proj",
    )(st.x, st.mod, w_main, w_small)


def _attn_kernel(sink_ref, qt_ref, *refs, local, nblk):
    if local:
        kp_ref, kc_ref, kn_ref, vp_ref, vc_ref, vn_ref, kx_ref, vx_ref, o_ref = refs
    else:
        kx_ref, vx_ref, o_ref = refs
    j = pl.program_id(1)
    grp = A_HEADS // A_KV_HEADS
    hd = A_HEAD_DIM
    blk = qt_ref.shape[2]
    nctx = kx_ref.shape[0]
    if local:
        span = 3 * blk
        shape = (span + nctx, grp * blk)
        kr = lax.broadcasted_iota(jnp.int32, shape, 0)
        qi = lax.broadcasted_iota(jnp.int32, shape, 1) & (blk - 1)
        rel = kr - qi
        lo = jnp.where(j > 0, 0, blk)
        hi = jnp.where(j < nblk - 1, span, 2 * blk)
        valid = (kr >= span) | ((rel >= 0) & (rel <= 2 * A_WINDOW) & (kr >= lo) & (kr < hi))
        bias = jnp.where(valid, 0.0, NEG_BIG)
    heads = list(range(A_KV_HEADS))
    hs = [slice(h * hd, (h + 1) * hd) for h in heads]
    qt = [jnp.concatenate([qt_ref[0, (h * grp + g) * hd:(h * grp + g + 1) * hd, :] for g in range(grp)], axis=1)
          for h in heads]
    snk = [jnp.concatenate([jnp.full((1, blk), sink_ref[h * grp + g], F32) for g in range(grp)], axis=1) for h in heads]
    if local:
        k_all = [jnp.concatenate([kp_ref[:, s], kc_ref[:, s], kn_ref[:, s], kx_ref[:, s]], axis=0) for s in hs]
        vt_all = [jnp.concatenate([vp_ref[0, s, :], vc_ref[0, s, :], vn_ref[0, s, :], vx_ref[0, s, :]], axis=1) for s in hs]
        s_t = _each(lambda a, b: _dot(a, b) + bias, k_all, qt)
    else:
        k_all = [kx_ref[:, s] for s in hs]
        vt_all = [vx_ref[0, s, :] for s in hs]
        s_t = _each(_dot, k_all, qt)
    m = _each(lambda s, k: jnp.maximum(jnp.max(s, 0, keepdims=True), k), s_t, snk)
    p = _each(lambda s, a: jnp.exp(s - a), s_t, m)
    den = _each(lambda pp, k, a: jnp.sum(pp, 0, keepdims=True) + jnp.exp(k - a), p, snk, m)
    acc = _each(lambda v, pp: _dot(v, pp.astype(BF16)), vt_all, p)
    for h, a, dn in zip(heads, acc, den):
        o = a / dn
        for g in range(grp):
            r0 = (h * grp + g) * hd
            o_ref[0, r0:r0 + hd, :] = o[:, g * blk:(g + 1) * blk].astype(BF16)


def _attention(qt, k, vt, k_ctx, vt_ctx, sink, nseq, lseq, lctx, local):
    blk = A_BLOCK
    nblk = lseq // blk
    nq = A_HEADS * A_HEAD_DIM
    nkv = A_KV_HEADS * A_HEAD_DIM
    nbrs = (lambda j: jnp.maximum(j - 1, 0), lambda j: j, lambda j: jnp.minimum(j + 1, nblk - 1))

    in_specs = [pl.BlockSpec(memory_space=pltpu.SMEM), pl.BlockSpec((1, nq, blk), lambda b, j: (b, 0, j))]
    args = [sink, qt]
    if local:
        for f in nbrs:
            in_specs.append(pl.BlockSpec((blk, nkv), (lambda f: lambda b, j: (b * nblk + f(j), 0))(f)))
            args.append(k)
        for f in nbrs:
            in_specs.append(pl.BlockSpec((1, nkv, blk), (lambda f: lambda b, j: (b, 0, f(j)))(f)))
            args.append(vt)
    in_specs += [pl.BlockSpec((lctx, nkv), lambda b, j: (b, 0)), pl.BlockSpec((1, nkv, lctx), lambda b, j: (b, 0, 0))]
    args += [k_ctx, vt_ctx]
    return pl.pallas_call(
        functools.partial(_attn_kernel, local=local, nblk=nblk),
        grid=(nseq, nblk), in_specs=in_specs,
        out_specs=pl.BlockSpec((1, nq, blk), lambda b, j: (b, 0, j)),
        out_shape=jax.ShapeDtypeStruct((nseq, nq, lseq), BF16),
        compiler_params=_cparams(2, 32), name="attn_local" if local else "attn_ctx",
    )(*args)


def _each(f, *lists):
    return [f(*a) for a in zip(*lists)]


def _unit_lower_inverse_minus_identity(lmats, ri, ci):
    b16 = (ri >> 4) == (ci >> 4)
    b32 = (ri >> 5) == (ci >> 5)

    def mm(a, b):
        return _dot(a.astype(BF16), b.astype(BF16))

    l_bd = _each(lambda l: jnp.where(b16, l, 0.0), lmats)
    m2 = _each(mm, l_bd, l_bd)
    m4 = _each(mm, m2, m2)
    p = _each(lambda l, a: a - l - mm(l, a), l_bd, m2)
    m8 = _each(mm, m4, m4)
    p = _each(lambda x, a: x + a + mm(x, a), p, m4)
    p = _each(lambda x, a: x + a + mm(x, a), p, m8)
    for sel in (lambda l: jnp.where(b32 & ~b16, l, 0.0), lambda l: jnp.where(b32, 0.0, l)):
        lo = _each(sel, lmats)
        y = _each(lambda x, a: a + mm(x, a), p, lo)
        p = _each(lambda x, a: x - (a + mm(a, x)), p, y)
    return p


def _gdn_scan_kernel(qf_ref, kf_ref, vf_ref, bgf_ref, qb_ref, kb_ref, vb_ref, bgb_ref, s0_ref,
                     of_ref, ob_ref, s_ref):
    c = pl.program_id(1)

    @pl.when(c == 0)
    def _():
        s_ref[...] = s0_ref[...]

    n = CHUNK
    ri = lax.broadcasted_iota(jnp.int32, (n, n), 0)
    ci = lax.broadcasted_iota(jnp.int32, (n, n), 1)
    zpad = jnp.zeros((n, LANES), F32)
    dirs = ((qf_ref, kf_ref, vf_ref, bgf_ref, of_ref, ri >= ci, ri > ci, n - 1),
            (qb_ref, kb_ref, vb_ref, bgb_ref, ob_ref, ri <= ci, ri < ci, 0))
    cums = []
    for d in range(2):
        bg = dirs[d][3][0]
        gc_all = _dot_sel(dirs[d][5].astype(BF16), bg)
        cums.append((bg, gc_all, jnp.concatenate([gc_all, zpad], axis=0).T))

    units = [(d, h) for h in range(B_HEADS) for d in range(2)]
    incl = [dirs[d][5] for d, _ in units]
    strict = [dirs[d][6] for d, _ in units]
    q = [dirs[d][0][0, h].astype(F32) for d, h in units]
    k = [dirs[d][1][0, h].astype(F32) for d, h in units]
    v = [dirs[d][2][0, h].astype(F32) for d, h in units]
    beta = [cums[d][0][:, d * B_HEADS + h:d * B_HEADS + h + 1] for d, h in units]
    gcol = [(2 + d) * B_HEADS + h for d, h in units]
    gc = [cums[d][1][:, g:g + 1] for (d, _), g in zip(units, gcol)]
    gc_row = [cums[d][2][g:g + 1, :n] for (d, _), g in zip(units, gcol)]
    gl = [a[dirs[d][7]:dirs[d][7] + 1, :] for (d, _), a in zip(units, gc)]
    decay = _each(lambda m, a, b: jnp.where(m, jnp.exp(jnp.where(m, a - b, 0.0)), 0.0), incl, gc, gc_row)
    kbeta = _each(lambda a, b: a * b, k, beta)
    k_bf = _each(lambda a: a.astype(BF16), k)
    lmat = _each(lambda m, a, b, dc: jnp.where(m, _dot_nt(a.astype(BF16), b) * dc, 0.0), strict, kbeta, k_bf, decay)
    a_qk = _each(lambda m, a, b, dc: jnp.where(m, _dot_nt(a.astype(BF16), b) * dc, 0.0), incl, q, k_bf, decay)
    tp = _unit_lower_inverse_minus_identity(lmat, ri, ci)
    egc = _each(jnp.exp, gc)
    rhs = _each(lambda a, b, kb, e: jnp.concatenate([a * b, kb * e], axis=1), v, beta, kbeta, egc)
    uw = _each(lambda r, t: r + _dot(t.astype(BF16), r.astype(BF16)), rhs, tp)
    kdec_t = _each(lambda a, t, c: jnp.concatenate([a * jnp.exp(t - c), zpad], axis=0).T.astype(BF16), k, gl, gc)
    qdec = _each(lambda a, e: (a * e).astype(BF16), q, egc)
    s = [s_ref[0, d * B_HEADS + h] for d, h in units]
    s_bf = _each(lambda a: a.astype(BF16), s)
    vn_bf = _each(lambda a, b: (a[:, :B_HEAD_DIM] - _dot(a[:, B_HEAD_DIM:].astype(BF16), b)).astype(BF16), uw, s_bf)
    o = _each(lambda a, b, m, w: _dot(a, b) + _dot(m.astype(BF16), w), qdec, s_bf, a_qk, vn_bf)
    bpad = jnp.zeros((n, LANES), BF16)
    s_new = _each(lambda a, t, kt, w: a * jnp.exp(t) + _dot(kt, jnp.concatenate([w, bpad], axis=0)), s, gl, kdec_t, vn_bf)
    for (d, h), o_u, s_u in zip(units, o, s_new):
        dirs[d][4][0, h] = o_u
        s_ref[0, d * B_HEADS + h] = s_u


def _gdn_scan(q, k, v, bg, s0, nseq, lseq):
    nc = lseq // CHUNK
    bg3 = bg.reshape(nseq, lseq, LANES)
    hspec_f = pl.BlockSpec((1, B_HEADS, CHUNK, LANES), lambda b, c: (b, 0, c, 0))
    hspec_b = pl.BlockSpec((1, B_HEADS, CHUNK, LANES), lambda b, c: (b, 0, nc - 1 - c, 0))
    gspec_f = pl.BlockSpec((1, CHUNK, LANES), lambda b, c: (b, c, 0))
    gspec_b = pl.BlockSpec((1, CHUNK, LANES), lambda b, c: (b, nc - 1 - c, 0))
    sspec = pl.BlockSpec((1, 2 * B_HEADS, B_HEAD_DIM, B_HEAD_DIM), lambda b, c: (b, 0, 0, 0))
    oshape = jax.ShapeDtypeStruct((nseq, B_HEADS, lseq, LANES), F32)
    return pl.pallas_call(
        _gdn_scan_kernel, grid=(nseq, nc),
        in_specs=[hspec_f, hspec_f, hspec_f, gspec_f, hspec_b, hspec_b, hspec_b, gspec_b, sspec],
        out_specs=[hspec_f, hspec_b, sspec],
        out_shape=[oshape, oshape, jax.ShapeDtypeStruct(s0.shape, F32)],
        compiler_params=_cparams(2, 32), name="gdn_scan",
    )(q, k, v, bg3, q, k, v, bg3, s0)


def _gla_level_mats():
    n = CHUNK
    mats = []
    s = n // 2
    while s >= 1:
        m = np.zeros((n, n), np.float32)
        for i in range(n):
            ref = (i // (2 * s)) * 2 * s + s
            if i >= ref:
                m[i, ref + 1:i + 1] = 1.0
            else:
                m[i, i + 1:ref + 1] = -1.0
        mats.append(m)
        s //= 2
    mats.append(np.tril(np.ones((n, n), np.float32)))
    mats.append(np.triu(np.ones((n, n), np.float32), 1))
    fwd = np.concatenate(mats, 0)
    bwd = np.concatenate([m[::-1, ::-1] for m in mats], 0)
    return fwd, bwd


N_LEVELS = int(math.log2(CHUNK))


def _gla_scan_kernel(qf_ref, kf_ref, vf_ref, lowf_ref, qb_ref, kb_ref, vb_ref, lowb_ref, w2_ref, gb_ref,
                     mf_ref, mb_ref, s0_ref, of_ref, ob_ref, s_ref):
    c = pl.program_id(1)

    @pl.when(c == 0)
    def _():
        s_ref[...] = s0_ref[...]

    n = CHUNK
    dk_all = C_HEADS * C_KEY_DIM
    ri = lax.broadcasted_iota(jnp.int32, (n, n), 0)
    ci = lax.broadcasted_iota(jnp.int32, (n, n), 1)
    dirs = ((qf_ref, kf_ref, vf_ref, lowf_ref, mf_ref, of_ref, n - 1), (qb_ref, kb_ref, vb_ref, lowb_ref, mb_ref, ob_ref, 0))

    levs = []
    for d in range(2):
        low_ref, m_ref = dirs[d][3], dirs[d][4]
        logit = jnp.dot(low_ref[0], w2_ref[:, d * dk_all:(d + 1) * dk_all], precision=lax.Precision.HIGHEST,
                        preferred_element_type=F32) + gb_ref[:, d * dk_all:(d + 1) * dk_all]
        log_a = (jnp.minimum(logit, 0.0) - jnp.log1p(jnp.exp(-jnp.abs(logit)))) * (1.0 / C_GATE_TAU)
        levs.append(_dot_sel(m_ref[...], log_a, parts=2))

    units = [(d, h) for h in range(C_HEADS) for d in range(2)]

    def lev(blk):
        return [levs[d][blk * n:(blk + 1) * n, h * C_KEY_DIM:(h + 1) * C_KEY_DIM] for d, h in units]

    q = [dirs[d][0][0, h].astype(F32) for d, h in units]
    k = [dirs[d][1][0, h].astype(F32) for d, h in units]
    v_bf = [dirs[d][2][0, h] for d, h in units]
    a = _each(lambda x, y: jnp.where(ri == ci, _dot_nt(x.astype(BF16), y.astype(BF16)), 0.0), q, k)
    for lvl in range(N_LEVELS):
        sh = N_LEVELS - 1 - lvl
        same = (ri >> (sh + 1)) == (ci >> (sh + 1))
        hi_r, hi_c = ((ri >> sh) & 1) == 1, ((ci >> sh) & 1) == 1
        pair = (same & hi_r & ~hi_c, same & ~hi_r & hi_c)
        x = _each(lambda t: jnp.exp(-jnp.abs(t)), lev(lvl))
        a = [acc + jnp.where(pair[d], _dot_nt((qq * xx).astype(BF16), (kk * xx).astype(BF16)), 0.0)
             for (d, _), acc, qq, kk, xx in zip(units, a, q, k, x)]
    bcum = lev(N_LEVELS)
    rest = lev(N_LEVELS + 1)
    st = [s_ref[0, d * C_HEADS + h] for d, h in units]
    o = _each(lambda qq, b, s, aa, vv: _dot_nt((qq * jnp.exp(b)).astype(BF16), s.astype(BF16)) + _dot(aa.astype(BF16), vv),
              q, bcum, st, a, v_bf)
    kpad = jnp.zeros((n, C_KEY_DIM), BF16)
    vpad = jnp.zeros((n, C_VAL_DIM), F32)
    kdec = _each(lambda kk, r: jnp.concatenate([(kk * jnp.exp(r)).astype(BF16), kpad], axis=0), k, rest)
    v_t = _each(lambda vv: jnp.concatenate([vv.astype(F32), vpad], axis=0).T.astype(BF16), v_bf)
    s_new = [s * jnp.exp(b[dirs[d][6]:dirs[d][6] + 1, :]) + _dot(vt, kd)
             for (d, _), s, b, vt, kd in zip(units, st, bcum, v_t, kdec)]
    for (d, h), o_u, s_u in zip(units, o, s_new):
        dirs[d][5][0, h] = o_u
        s_ref[0, d * C_HEADS + h] = s_u


def _gla_scan(q, k, v, low, w2, gate_b, mats, s0, nseq, lseq):
    nc = lseq // CHUNK
    low3 = low.reshape(nseq, lseq, LANES)
    mf, mb = mats

    def hspec(width, rev):
        return pl.BlockSpec((1, C_HEADS, CHUNK, width), (lambda b, c: (b, 0, nc - 1 - c, 0)) if rev else (lambda b, c: (b, 0, c, 0)))

    def lspec(rev):
        return pl.BlockSpec((1, CHUNK, LANES), (lambda b, c: (b, nc - 1 - c, 0)) if rev else (lambda b, c: (b, c, 0)))

    def cspec(shape):
        nd = len(shape)
        return pl.BlockSpec(shape, lambda b, c: (0,) * nd)

    sspec = pl.BlockSpec((1, 2 * C_HEADS, C_VAL_DIM, C_KEY_DIM), lambda b, c: (b, 0, 0, 0))
    oshape = jax.ShapeDtypeStruct((nseq, C_HEADS, lseq, C_VAL_DIM), F32)
    return pl.pallas_call(
        _gla_scan_kernel, grid=(nseq, nc),
        in_specs=[hspec(C_KEY_DIM, False), hspec(C_KEY_DIM, False), hspec(C_VAL_DIM, False), lspec(False),
                  hspec(C_KEY_DIM, True), hspec(C_KEY_DIM, True), hspec(C_VAL_DIM, True), lspec(True),
                  cspec(w2.shape), cspec(gate_b.shape), cspec(mf.shape), cspec(mb.shape), sspec],
        out_specs=[hspec(C_VAL_DIM, False), hspec(C_VAL_DIM, True), sspec],
        out_shape=[oshape, oshape, jax.ShapeDtypeStruct(s0.shape, F32)],
        compiler_params=_cparams(2, 32), name="gla_scan",
    )(q, k, v, low3, q, k, v, low3, w2, gate_b, mf, mb, s0)


def _out_proj_kernel(*refs, heads, head_dim):
    if heads:
        of_ref, ob_ref, gate_ref, nw_ref, w_ref, x_ref, mod_ref, g_ref, b_ref, o_ref = refs
        parts = []
        for h in range(heads):
            o = of_ref[0, h] + ob_ref[0, h]
            o = o * lax.rsqrt(jnp.mean(o * o, -1, keepdims=True) + NORM_EPS) * nw_ref[...]
            gate = gate_ref[:, h * head_dim:(h + 1) * head_dim].astype(F32)
            parts.append((o * _silu(gate)).astype(BF16))
        inp = jnp.concatenate(parts, axis=1)
    else:
        inp_ref, w_ref, x_ref, mod_ref, g_ref, b_ref, o_ref = refs
        inp = inp_ref[0].astype(F32).T.astype(BF16)
    y = _dot(inp, w_ref[...])
    t = DN_ALPHA * x_ref[...] + mod_ref[0, 2:3, :] * y
    o_ref[...] = _layer_norm(t, g_ref[...], b_ref[...])


def _out_proj(st, w_bf, ln_g, ln_b, inp=None, scan_out=None, gate=None, gate_col=0, norm_w=None):
    common_specs = [_const_spec(w_bf.shape), st.row_spec(D_MODEL), st.mod_spec(),
                    _const_spec((1, D_MODEL)), _const_spec((1, D_MODEL))]
    common_args = [w_bf, st.x, st.mod, ln_g.reshape(1, -1), ln_b.reshape(1, -1)]
    if scan_out is None:
        heads = head_dim = 0
        tps = st.tps
        in_specs = [pl.BlockSpec((1, inp.shape[1], st.tm), lambda i: (i // tps, 0, i % tps))] + common_specs
        args = [inp] + common_args
    else:
        o_f, o_b = scan_out
        heads, head_dim = o_f.shape[1], o_f.shape[3]
        in_specs = [st.head_spec(heads, head_dim), st.head_spec(heads, head_dim),
                    st.row_spec(heads * head_dim, gate_col), _const_spec((1, head_dim))] + common_specs
        args = [o_f, o_b, gate, norm_w.reshape(1, -1)] + common_args
    return pl.pallas_call(
        functools.partial(_out_proj_kernel, heads=heads, head_dim=head_dim),
        grid=st.grid, in_specs=in_specs, out_specs=st.row_spec(D_MODEL),
        out_shape=jax.ShapeDtypeStruct((st.rows, D_MODEL), F32),
        compiler_params=_cparams(1, 40), name="out_proj",
    )(*args)


def _ffn_kernel(x_ref, xp_ref, xn_ref, mod_ref, wu_ref, cw_ref, cb_ref, wd_ref, g_ref, b_ref, o_ref,
                h_ref, act_ref, *, tps):
    t = pl.program_id(0) % tps
    tm = x_ref.shape[0]
    pad = SUBLANES
    x = x_ref[...]
    h_ref[pad:pad + tm, :] = _modulated(x, mod_ref, 3).astype(BF16)
    h_ref[0:pad, :] = (_modulated(xp_ref[...], mod_ref, 3) * (t > 0).astype(F32)).astype(BF16)
    h_ref[pad + tm:, :] = (_modulated(xn_ref[...], mod_ref, 3) * (t < tps - 1).astype(F32)).astype(BF16)
    for c0 in range(0, D_FF, FF_CHUNK):
        val = _dot(h_ref[pad:pad + tm, :], wu_ref[:, c0:c0 + FF_CHUNK])
        gx = _dot(h_ref[...], wu_ref[:, D_FF + c0:D_FF + c0 + FF_CHUNK])
        gt = (cw_ref[0:1, c0:c0 + FF_CHUNK] * pltpu.roll(gx, 1, 0)[pad:pad + tm]
              + cw_ref[1:2, c0:c0 + FF_CHUNK] * gx[pad:pad + tm]
              + cw_ref[2:3, c0:c0 + FF_CHUNK] * pltpu.roll(gx, tm + 2 * pad - 1, 0)[pad:pad + tm]
              + cb_ref[:, c0:c0 + FF_CHUNK])
        act_ref[:, c0:c0 + FF_CHUNK] = (_silu(gt) * val).astype(BF16)
    y = _dot(act_ref[...], wd_ref[...])
    o_ref[...] = _layer_norm(DN_ALPHA * x + mod_ref[0, 5:6, :] * y, g_ref[...], b_ref[...])


def _ffn(st, w_up, conv_w, conv_b, w_down, ln_g, ln_b):
    tm, tps = st.tm, st.tps
    r8 = tm // SUBLANES
    nb8 = st.rows // SUBLANES
    return pl.pallas_call(
        functools.partial(_ffn_kernel, tps=tps), grid=st.grid,
        in_specs=[st.row_spec(D_MODEL),
                  pl.BlockSpec((SUBLANES, D_MODEL), lambda i: (jnp.maximum(i * r8 - 1, 0), 0)),
                  pl.BlockSpec((SUBLANES, D_MODEL), lambda i: (jnp.minimum((i + 1) * r8, nb8 - 1), 0)),
                  st.mod_spec(), _const_spec(w_up.shape), _const_spec(conv_w.shape), _const_spec((1, D_FF)),
                  _const_spec(w_down.shape), _const_spec((1, D_MODEL)), _const_spec((1, D_MODEL))],
        out_specs=st.row_spec(D_MODEL),
        out_shape=jax.ShapeDtypeStruct((st.rows, D_MODEL), F32),
        scratch_shapes=[pltpu.VMEM((tm + 2 * SUBLANES, D_MODEL), BF16), pltpu.VMEM((tm, D_FF), BF16)],
        compiler_params=_cparams(1, 56), name="ffn",
    )(st.x, st.x, st.x, st.mod, w_up, conv_w, conv_b.reshape(1, -1), w_down, ln_g.reshape(1, -1), ln_b.reshape(1, -1))


def _rope_tables(n):
    rows = n // GRID_W
    r = jnp.repeat(jnp.arange(rows), GRID_W).astype(F32)
    col = jnp.tile(jnp.arange(GRID_W), rows).astype(F32)
    n_freq = A_HEAD_DIM // 4
    inv = jnp.power(ROPE_BASE, -jnp.arange(n_freq, dtype=F32) / n_freq)
    ang = jnp.concatenate([r[:, None] * inv, col[:, None] * inv], -1)
    cos, sin = jnp.cos(ang), jnp.sin(ang)
    zero = jnp.zeros_like(sin)
    reps = LANES // A_HEAD_DIM
    c = jnp.tile(jnp.concatenate([cos, cos], -1), (1, reps))
    s1 = jnp.tile(jnp.concatenate([-sin, zero], -1), (1, reps))
    s2 = jnp.tile(jnp.concatenate([zero, sin], -1), (1, reps))
    return c, s1, s2


def _pad_cols(w, width):
    return jnp.pad(w, ((0, 0), (0, width - w.shape[1])))


def kernel(x, c, ctx, c_ctx, ada_w, ada_b, ln_g, ln_b, ffn_w_up, ffn_conv_w, ffn_conv_b, ffn_w_down, attn_w_qkv, attn_sink, attn_w_o, gdn_w_in, gdn_conv_w, gdn_a_log, gdn_dt_bias, gdn_norm_w, gdn_w_o, gla_w_in, gla_w_gate2, gla_gate_b, gla_norm_w, gla_w_o):
    bsz, n, d = x.shape
    nctx = ctx.shape[1]
    assert d == D_MODEL and n % A_BLOCK == 0 and n % GRID_W == 0 and nctx % CHUNK == 0 and n % CHUNK == 0

    cond_rows = -(-(bsz + 1) // SUBLANES) * SUBLANES
    cond = jnp.zeros((cond_rows, d), F32).at[:bsz].set(c).at[bsz].set(c_ctx)
    mod_all = _ada(cond, ada_w, ada_b)

    rope_tabs = _rope_tables(n)
    gla_mats = tuple(jnp.asarray(m, BF16) for m in _gla_level_mats())

    xl = x.reshape(bsz * n, d)
    xc = ctx.reshape(bsz * nctx, d)
    for i in range(DEPTH):
        need_ctx_out = i < DEPTH - 1
        lat = _Stream(xl, mod_all[i, :bsz].reshape(bsz, 6, d), bsz, n)
        cx = _Stream(xc, mod_all[i, bsz:bsz + 1].reshape(1, 6, d), bsz, nctx)
        kind, slot = i % N_MIXERS, i // N_MIXERS
        if kind == 0:
            w_qkv = attn_w_qkv[slot].astype(BF16)
            w_o = attn_w_o[slot].astype(BF16)
            qt_c, k_c, vt_c = _attn_proj(cx, w_qkv, None)
            qt_l, k_l, vt_l = _attn_proj(lat, w_qkv, rope_tabs)
            mix_l = dict(inp=_attention(qt_l, k_l, vt_l, k_c, vt_c, attn_sink[slot], bsz, n, nctx, True))
            if need_ctx_out:
                mix_c = dict(inp=_attention(qt_c, None, None, k_c, vt_c, attn_sink[slot], bsz, nctx, nctx, False))
        elif kind == 1:
            hk = B_HEADS * B_HEAD_DIM
            w_in = gdn_w_in[slot]
            w_main = w_in[:, :4 * hk].astype(BF16)
            w_small = _pad_cols(w_in[:, 4 * hk:], LANES).astype(BF16)
            w_o = gdn_w_o[slot].astype(BF16)
            a_vec = jnp.zeros((1, LANES), F32).at[0, 2 * B_HEADS:4 * B_HEADS].set(jnp.exp(gdn_a_log[slot]).reshape(-1))
            dt_vec = jnp.zeros((1, LANES), F32).at[0, 2 * B_HEADS:4 * B_HEADS].set(gdn_dt_bias[slot].reshape(-1))
            state = jnp.zeros((bsz, 2 * B_HEADS, B_HEAD_DIM, B_HEAD_DIM), F32)
            outs = []
            for st in (cx, lat):
                q, k, v, gate, bg = _gdn_proj(st, w_main, w_small, gdn_conv_w[slot], a_vec, dt_vec)
                o_f, o_b, state = _gdn_scan(q, k, v, bg, state, st.nseq, st.lseq)
                outs.append(dict(scan_out=(o_f, o_b), gate=gate, norm_w=gdn_norm_w[slot]))
            mix_c, mix_l = outs
        else:
            dk, dv = C_HEADS * C_KEY_DIM, C_HEADS * C_VAL_DIM
            w_in = gla_w_in[slot]
            w_main = w_in[:, :2 * dk + 2 * dv].astype(BF16)
            w_small = _pad_cols(w_in[:, 2 * dk + 2 * dv:], LANES).astype(BF16)
            w_o = gla_w_o[slot].astype(BF16)
            w2 = jnp.zeros((LANES, 2 * dk), F32)
            for z in range(2):
                w2 = w2.at[z * C_GATE_RANK:(z + 1) * C_GATE_RANK, z * dk:(z + 1) * dk].set(gla_w_gate2[slot, z])
            gate_b = gla_gate_b[slot].reshape(1, 2 * dk)
            state = jnp.zeros((bsz, 2 * C_HEADS, C_VAL_DIM, C_KEY_DIM), F32)
            outs = []
            for st in (cx, lat):
                q, k, v, gate, low = _gla_proj(st, w_main, w_small)
                o_f, o_b, state = _gla_scan(q, k, v, low, w2, gate_b, gla_mats, state, st.nseq, st.lseq)
                outs.append(dict(scan_out=(o_f, o_b), gate=gate, norm_w=gla_norm_w[slot]))
            mix_c, mix_l = outs

        w_up = ffn_w_up[i].astype(BF16)
        w_down = ffn_w_down[i].astype(BF16)
        streams = [(lat, mix_l)] + ([(cx, mix_c)] if need_ctx_out else [])
        new = []
        for st, mix in streams:
            mid = _out_proj(st, w_o, ln_g[i, 0], ln_b[i, 0], **mix)
            st_mid = _Stream(mid, st.mod, st.nseq, st.lseq)
            new.append(_ffn(st_mid, w_up, ffn_conv_w[i], ffn_conv_b[i], w_down, ln_g[i, 1], ln_b[i, 1]))
        xl = new[0]
        if need_ctx_out:
            xc = new[1]
    return xl.reshape(bsz, n, d)
```

```python
import functools
import math

import numpy as np
import jax
import jax.numpy as jnp
from jax import lax
from jax.experimental import pallas as pl
from jax.experimental.pallas import tpu as pltpu

F32 = jnp.float32
BF16 = jnp.bfloat16

D_MODEL = 1024
DEPTH = 4
GRID_W = 64
N_MIXERS = 3
D_FF = 2816
A_HEADS = 16
A_KV_HEADS = 4
A_HEAD_DIM = 64
A_WINDOW = 128
A_BLOCK = 128
ROPE_BASE = 10000.0
B_HEADS = 8
B_HEAD_DIM = 128
C_HEADS = 4
C_KEY_DIM = 128
C_VAL_DIM = 256
C_GATE_RANK = 16
C_GATE_TAU = 16.0
CHUNK = 64
NORM_EPS = 1e-5
DN_ALPHA = (2 * DEPTH) ** 0.25

LANES = 128
SUBLANES = 8
V7X_VMEM_BYTES = 64 * 1024 * 1024
MIB = 1024 * 1024

ROW_TILE = 512
FF_CHUNK = 256
SCAN_SEQS = 2
NEG_BIG = -1e30
LOG2E = math.log2(math.e)


def _cparams(n_axes, vmem_mib):
    return pltpu.CompilerParams(dimension_semantics=("arbitrary",) * n_axes,
                                vmem_limit_bytes=min(vmem_mib * MIB, V7X_VMEM_BYTES - 8 * MIB))


def _row_tile(lseq, target=ROW_TILE):
    t = min(lseq, target)
    while lseq % t:
        t -= SUBLANES
    return t


def _dot(a, b):
    return jnp.dot(a, b, preferred_element_type=F32)


def _dot_nt(a, b):
    return lax.dot_general(a, b, (((1,), (1,)), ((), ())), preferred_element_type=F32)


def _split3(x):
    hi = x.astype(BF16)
    r = x - hi.astype(F32)
    mid = r.astype(BF16)
    lo = (r - mid.astype(F32)).astype(BF16)
    return hi, mid, lo


def _dot_sel(m_bf, x, parts=3):
    pieces = _split3(x)[:parts]
    acc = _dot(m_bf, pieces[0])
    for piece in pieces[1:]:
        acc = acc + _dot(m_bf, piece)
    return acc


def _silu(x):
    return x * jax.nn.sigmoid(x)


def _softplus(x):
    return jnp.maximum(x, 0.0) + jnp.log1p(jnp.exp(-jnp.abs(x)))


def _layer_norm(t, g, b):
    mu = jnp.mean(t, -1, keepdims=True)
    d = t - mu
    var = jnp.mean(d * d, -1, keepdims=True)
    return d * lax.rsqrt(var + NORM_EPS) * g + b


def _modulated(x, mod_ref, shift_row):
    sh = mod_ref[0, shift_row:shift_row + 1, :]
    sc = mod_ref[0, shift_row + 1:shift_row + 2, :]
    return x * (1.0 + sc) + sh


def _ada_kernel(s_ref, w_ref, b_ref, o_ref):
    s = _silu(s_ref[...])
    o_ref[0] = jnp.dot(s, w_ref[0], precision=lax.Precision.HIGHEST, preferred_element_type=F32) + b_ref[0]


def _ada(cond, ada_w, ada_b):
    depth, d, n = ada_w.shape
    rows = cond.shape[0]
    tn = 1536
    return pl.pallas_call(
        _ada_kernel,
        grid=(depth, n // tn),
        in_specs=[pl.BlockSpec((rows, d), lambda i, j: (0, 0)),
                  pl.BlockSpec((1, d, tn), lambda i, j: (i, 0, j)),
                  pl.BlockSpec((1, 1, tn), lambda i, j: (i, 0, j))],
        out_specs=pl.BlockSpec((1, rows, tn), lambda i, j: (i, 0, j)),
        out_shape=jax.ShapeDtypeStruct((depth, rows, n), F32),
        compiler_params=_cparams(2, 40),
        name="ada",
    )(cond, ada_w, ada_b.reshape(depth, 1, n))


class _Stream:
    def __init__(self, x, mod, nseq, lseq):
        self.x, self.mod, self.nseq, self.lseq = x, mod, nseq, lseq
        self.tm = _row_tile(lseq)
        self.tps = lseq // self.tm
        self.shared_mod = mod.shape[0] == 1

    def mod_spec(self):
        d = self.mod.shape[-1]
        if self.shared_mod:
            return pl.BlockSpec((1, 6, d), lambda i: (0, 0, 0))
        tps = self.tps
        return pl.BlockSpec((1, 6, d), lambda i: (i // tps, 0, 0))

    def row_spec(self, width, col_block=0):
        return pl.BlockSpec((self.tm, width), lambda i: (i, col_block))

    def head_spec(self, heads, width):
        tps = self.tps
        return pl.BlockSpec((1, heads, self.tm, width), lambda i: (i // tps, 0, i % tps, 0))

    def head_shape(self, heads, width, dtype):
        return jax.ShapeDtypeStruct((self.nseq, heads, self.lseq, width), dtype)

    @property
    def rows(self):
        return self.nseq * self.lseq

    @property
    def grid(self):
        return (self.rows // self.tm,)


def _const_spec(shape):
    nd = len(shape)
    return pl.BlockSpec(shape, lambda i: (0,) * nd, pipeline_mode=pl.Buffered(1))


def _attn_proj_kernel(x_ref, mod_ref, w_ref, *refs, rope):
    qt_ref, k_ref, vt_ref = refs[-3:]
    u = _modulated(x_ref[...], mod_ref, 0).astype(BF16)
    z = _dot(u, w_ref[...])
    nq = A_HEADS * A_HEAD_DIM // LANES
    nk = A_KV_HEADS * A_HEAD_DIM // LANES
    for j in range(z.shape[1] // LANES):
        t = z[:, j * LANES:(j + 1) * LANES]
        if j < nq + nk and rope:
            c_ref, s1_ref, s2_ref = refs[:3]
            t = (t * c_ref[...] + pltpu.roll(t, LANES - A_HEAD_DIM // 2, 1) * s1_ref[...]
                 + pltpu.roll(t, A_HEAD_DIM // 2, 1) * s2_ref[...])
        if j < nq:
            qt_ref[0, j * LANES:(j + 1) * LANES, :] = (t * (A_HEAD_DIM ** -0.5 * LOG2E)).T.astype(BF16)
        elif j < nq + nk:
            k_ref[:, (j - nq) * LANES:(j - nq + 1) * LANES] = t.astype(BF16)
        else:
            vt_ref[0, (j - nq - nk) * LANES:(j - nq - nk + 1) * LANES, :] = t.T.astype(BF16)


def _attn_proj(st, w_bf, rope_tabs):
    nq = A_HEADS * A_HEAD_DIM
    nkv = A_KV_HEADS * A_HEAD_DIM
    tps = st.tps
    in_specs = [st.row_spec(D_MODEL), st.mod_spec(), _const_spec(w_bf.shape)]
    args = [st.x, st.mod, w_bf]
    if rope_tabs is not None:
        for t in rope_tabs:
            in_specs.append(pl.BlockSpec((st.tm, LANES), lambda i: (i % tps, 0)))
            args.append(t)

    def fm_spec(feat):
        return pl.BlockSpec((1, feat, st.tm), lambda i: (i // tps, 0, i % tps))

    return pl.pallas_call(
        functools.partial(_attn_proj_kernel, rope=rope_tabs is not None),
        grid=st.grid, in_specs=in_specs, out_specs=[fm_spec(nq), st.row_spec(nkv), fm_spec(nkv)],
        out_shape=[jax.ShapeDtypeStruct((st.nseq, nq, st.lseq), BF16), jax.ShapeDtypeStruct((st.rows, nkv), BF16),
                   jax.ShapeDtypeStruct((st.nseq, nkv, st.lseq), BF16)],
        compiler_params=_cparams(1, 40), name="attn_proj",
    )(*args)


def _gdn_proj_kernel(x_ref, xp_ref, xn_ref, mod_ref, w_ref, ws_ref, cw_ref, av_ref, dt_ref,
                     q_ref, k_ref, v_ref, gate_ref, bg_ref, u_ref, *, tps):
    t = pl.program_id(0) % tps
    tm = x_ref.shape[0]
    pad = SUBLANES
    hk = B_HEADS * B_HEAD_DIM
    u_ref[pad:pad + tm, :] = _modulated(x_ref[...], mod_ref, 0).astype(BF16)
    u_ref[0:pad, :] = (_modulated(xp_ref[...], mod_ref, 0) * (t > 0).astype(F32)).astype(BF16)
    u_ref[pad + tm:, :] = (_modulated(xn_ref[...], mod_ref, 0) * (t < tps - 1).astype(F32)).astype(BF16)
    width = 2 * LANES
    for c0 in range(0, 3 * hk, width):
        zx = _dot(u_ref[...], w_ref[:, c0:c0 + width])
        y = _silu(cw_ref[0:1, c0:c0 + width] * pltpu.roll(zx, 1, 0)[pad:pad + tm]
                  + cw_ref[1:2, c0:c0 + width] * zx[pad:pad + tm]
                  + cw_ref[2:3, c0:c0 + width] * pltpu.roll(zx, tm + 2 * pad - 1, 0)[pad:pad + tm])
        for j in range(c0 // LANES, (c0 + width) // LANES):
            yj = y[:, j * LANES - c0:(j + 1) * LANES - c0]
            if j < 2 * B_HEADS:
                yj = yj * lax.rsqrt(jnp.sum(yj * yj, -1, keepdims=True) + 1e-6)
            if j < B_HEADS:
                q_ref[0, j] = (yj * (B_HEAD_DIM ** -0.5)).astype(BF16)
            elif j < 2 * B_HEADS:
                k_ref[0, j - B_HEADS] = yj.astype(BF16)
            else:
                v_ref[0, j - 2 * B_HEADS] = yj.astype(BF16)
    u = u_ref[pad:pad + tm, :]
    gate_ref[...] = _dot(u, w_ref[:, 3 * hk:]).astype(BF16)
    zs = _dot(u, ws_ref[...])
    lane = lax.broadcasted_iota(jnp.int32, zs.shape, 1)
    bg_ref[...] = jnp.where(lane < 2 * B_HEADS, jax.nn.sigmoid(zs), -av_ref[...] * _softplus(zs + dt_ref[...]))


def _gdn_proj(st, w_main, w_small, conv_w, a_vec, dt_vec):
    hk = B_HEADS * B_HEAD_DIM
    tm, tps = st.tm, st.tps
    r8 = tm // SUBLANES
    nb8 = st.rows // SUBLANES
    hspec = st.head_spec(B_HEADS, LANES)
    hshape = st.head_shape(B_HEADS, LANES, BF16)
    return pl.pallas_call(
        functools.partial(_gdn_proj_kernel, tps=tps), grid=st.grid,
        in_specs=[st.row_spec(D_MODEL),
                  pl.BlockSpec((SUBLANES, D_MODEL), lambda i: (jnp.maximum(i * r8 - 1, 0), 0)),
                  pl.BlockSpec((SUBLANES, D_MODEL), lambda i: (jnp.minimum((i + 1) * r8, nb8 - 1), 0)),
                  st.mod_spec(), _const_spec(w_main.shape), _const_spec(w_small.shape), _const_spec(conv_w.shape),
                  _const_spec(a_vec.shape), _const_spec(dt_vec.shape)],
        out_specs=[hspec, hspec, hspec, st.row_spec(hk), st.row_spec(LANES)],
        out_shape=[hshape, hshape, hshape, jax.ShapeDtypeStruct((st.rows, hk), BF16),
                   jax.ShapeDtypeStruct((st.rows, LANES), F32)],
        scratch_shapes=[pltpu.VMEM((tm + 2 * SUBLANES, D_MODEL), BF16)],
        compiler_params=_cparams(1, 48), name="gdn_proj",
    )(st.x, st.x, st.x, st.mod, w_main, w_small, conv_w, a_vec, dt_vec)


def _gla_proj_kernel(x_ref, mod_ref, w_ref, ws_ref, q_ref, k_ref, v_ref, gate_ref, low_ref):
    u = _modulated(x_ref[...], mod_ref, 0).astype(BF16)
    z = _dot(u, w_ref[...])
    dk, dv = C_HEADS * C_KEY_DIM, C_HEADS * C_VAL_DIM
    for h in range(C_HEADS):
        q_ref[0, h] = (z[:, h * C_KEY_DIM:(h + 1) * C_KEY_DIM] * (C_KEY_DIM ** -0.5)).astype(BF16)
        k_ref[0, h] = z[:, dk + h * C_KEY_DIM:dk + (h + 1) * C_KEY_DIM].astype(BF16)
        v_ref[0, h] = z[:, 2 * dk + h * C_VAL_DIM:2 * dk + (h + 1) * C_VAL_DIM].astype(BF16)
    gate_ref[...] = z[:, 2 * dk + dv:].astype(BF16)
    low_ref[...] = _dot(u, ws_ref[...])


def _gla_proj(st, w_main, w_small):
    dv = C_HEADS * C_VAL_DIM
    return pl.pallas_call(
        _gla_proj_kernel, grid=st.grid,
        in_specs=[st.row_spec(D_MODEL), st.mod_spec(), _const_spec(w_main.shape), _const_spec(w_small.shape)],
        out_specs=[st.head_spec(C_HEADS, C_KEY_DIM), st.head_spec(C_HEADS, C_KEY_DIM),
                   st.head_spec(C_HEADS, C_VAL_DIM), st.row_spec(dv), st.row_spec(LANES)],
        out_shape=[st.head_shape(C_HEADS, C_KEY_DIM, BF16), st.head_shape(C_HEADS, C_KEY_DIM, BF16),
                   st.head_shape(C_HEADS, C_VAL_DIM, BF16), jax.ShapeDtypeStruct((st.rows, dv), BF16),
                   jax.ShapeDtypeStruct((st.rows, LANES), F32)],
        compiler_params=_cparams(1, 48), name="gla_proj",
    )(st.x, st.mod, w_main, w_small)


def _attn_kernel(sink_ref, qt_ref, *refs, local, nblk):
    if local:
        kp_ref, kc_ref, kn_ref, vp_ref, vc_ref, vn_ref, kx_ref, vx_ref, o_ref = refs
    else:
        kx_ref, vx_ref, o_ref = refs
    j = pl.program_id(1)
    grp = A_HEADS // A_KV_HEADS
    hd = A_HEAD_DIM
    blk = qt_ref.shape[2]
    nctx = kx_ref.shape[0]
    if local:
        span = 3 * blk
        shape = (span + nctx, grp * blk)
        kr = lax.broadcasted_iota(jnp.int32, shape, 0)
        qi = lax.broadcasted_iota(jnp.int32, shape, 1) & (blk - 1)
        rel = kr - qi
        lo = jnp.where(j > 0, 0, blk)
        hi = jnp.where(j < nblk - 1, span, 2 * blk)
        valid = (kr >= span) | ((rel >= 0) & (rel <= 2 * A_WINDOW) & (kr >= lo) & (kr < hi))
        bias = jnp.where(valid, 0.0, NEG_BIG)
    heads = list(range(A_KV_HEADS))
    hs = [slice(h * hd, (h + 1) * hd) for h in heads]
    qt = [jnp.concatenate([qt_ref[0, (h * grp + g) * hd:(h * grp + g + 1) * hd, :] for g in range(grp)], axis=1)
          for h in heads]
    snk = [jnp.concatenate([jnp.full((1, blk), sink_ref[h * grp + g] * LOG2E, F32) for g in range(grp)], axis=1)
           for h in heads]
    if local:
        k_all = [jnp.concatenate([kp_ref[:, s], kc_ref[:, s], kn_ref[:, s], kx_ref[:, s]], axis=0) for s in hs]
        vt_all = [jnp.concatenate([vp_ref[0, s, :], vc_ref[0, s, :], vn_ref[0, s, :], vx_ref[0, s, :]], axis=1) for s in hs]
        s_t = _each(lambda a, b: _dot(a, b) + bias, k_all, qt)
    else:
        k_all = [kx_ref[:, s] for s in hs]
        vt_all = [vx_ref[0, s, :] for s in hs]
        s_t = _each(_dot, k_all, qt)
    m = _each(lambda s, k: jnp.maximum(jnp.max(s, 0, keepdims=True), k), s_t, snk)
    p = _each(lambda s, a: jnp.exp2(s - a), s_t, m)
    den = _each(lambda pp, k, a: jnp.sum(pp, 0, keepdims=True) + jnp.exp2(k - a), p, snk, m)
    acc = _each(lambda v, pp: _dot(v, pp.astype(BF16)), vt_all, p)
    for h, a, dn in zip(heads, acc, den):
        o = a / dn
        for g in range(grp):
            r0 = (h * grp + g) * hd
            o_ref[0, r0:r0 + hd, :] = o[:, g * blk:(g + 1) * blk].astype(BF16)


def _attention(qt, k, vt, k_ctx, vt_ctx, sink, nseq, lseq, lctx, local):
    blk = A_BLOCK
    nblk = lseq // blk
    nq = A_HEADS * A_HEAD_DIM
    nkv = A_KV_HEADS * A_HEAD_DIM
    nbrs = (lambda j: jnp.maximum(j - 1, 0), lambda j: j, lambda j: jnp.minimum(j + 1, nblk - 1))

    in_specs = [pl.BlockSpec(memory_space=pltpu.SMEM), pl.BlockSpec((1, nq, blk), lambda b, j: (b, 0, j))]
    args = [sink, qt]
    if local:
        for f in nbrs:
            in_specs.append(pl.BlockSpec((blk, nkv), (lambda f: lambda b, j: (b * nblk + f(j), 0))(f)))
            args.append(k)
        for f in nbrs:
            in_specs.append(pl.BlockSpec((1, nkv, blk), (lambda f: lambda b, j: (b, 0, f(j)))(f)))
            args.append(vt)
    in_specs += [pl.BlockSpec((lctx, nkv), lambda b, j: (b, 0)), pl.BlockSpec((1, nkv, lctx), lambda b, j: (b, 0, 0))]
    args += [k_ctx, vt_ctx]
    return pl.pallas_call(
        functools.partial(_attn_kernel, local=local, nblk=nblk),
        grid=(nseq, nblk), in_specs=in_specs,
        out_specs=pl.BlockSpec((1, nq, blk), lambda b, j: (b, 0, j)),
        out_shape=jax.ShapeDtypeStruct((nseq, nq, lseq), BF16),
        compiler_params=_cparams(2, 32), name="attn_local" if local else "attn_ctx",
    )(*args)


def _each(f, *lists):
    return [f(*a) for a in zip(*lists)]


def _unit_lower_inverse_minus_identity(lmats, ri, ci):
    b16 = (ri >> 4) == (ci >> 4)
    b32 = (ri >> 5) == (ci >> 5)

    def mm(a, b):
        return _dot(a.astype(BF16), b.astype(BF16))

    l_bd = _each(lambda l: jnp.where(b16, l, 0.0), lmats)
    m2 = _each(mm, l_bd, l_bd)
    m4 = _each(mm, m2, m2)
    p = _each(lambda l, a: a - l - mm(l, a), l_bd, m2)
    m8 = _each(mm, m4, m4)
    p = _each(lambda x, a: x + a + mm(x, a), p, m4)
    p = _each(lambda x, a: x + a + mm(x, a), p, m8)
    for sel in (lambda l: jnp.where(b32 & ~b16, l, 0.0), lambda l: jnp.where(b32, 0.0, l)):
        lo = _each(sel, lmats)
        y = _each(lambda x, a: a + mm(x, a), p, lo)
        p = _each(lambda x, a: x - (a + mm(a, x)), p, y)
    return p


def _gdn_scan_kernel(qf_ref, kf_ref, vf_ref, bgf_ref, qb_ref, kb_ref, vb_ref, bgb_ref, s0_ref,
                     of_ref, ob_ref, s_ref):
    c = pl.program_id(1)

    @pl.when(c == 0)
    def _():
        s_ref[...] = s0_ref[...]

    n = CHUNK
    ri = lax.broadcasted_iota(jnp.int32, (n, n), 0)
    ci = lax.broadcasted_iota(jnp.int32, (n, n), 1)
    zpad = jnp.zeros((n, LANES), F32)
    dirs = ((qf_ref, kf_ref, vf_ref, bgf_ref, of_ref, ri >= ci, ri > ci, n - 1),
            (qb_ref, kb_ref, vb_ref, bgb_ref, ob_ref, ri <= ci, ri < ci, 0))
    nb = qf_ref.shape[0]
    cums = {}
    for bi in range(nb):
        for d in range(2):
            bg = dirs[d][3][bi]
            gc_all = _dot_sel(dirs[d][5].astype(BF16), bg)
            cums[bi, d] = (bg, gc_all, jnp.concatenate([gc_all, zpad], axis=0).T)

    units = [(bi, d, h) for h in range(B_HEADS) for d in range(2) for bi in range(nb)]
    incl = [dirs[d][5] for _, d, _ in units]
    strict = [dirs[d][6] for _, d, _ in units]
    q = [dirs[d][0][bi, h].astype(F32) for bi, d, h in units]
    k = [dirs[d][1][bi, h].astype(F32) for bi, d, h in units]
    v = [dirs[d][2][bi, h].astype(F32) for bi, d, h in units]
    beta = [cums[bi, d][0][:, d * B_HEADS + h:d * B_HEADS + h + 1] for bi, d, h in units]
    gcol = [(2 + d) * B_HEADS + h for _, d, h in units]
    gc = [cums[bi, d][1][:, g:g + 1] for (bi, d, _), g in zip(units, gcol)]
    gc_row = [cums[bi, d][2][g:g + 1, :n] for (bi, d, _), g in zip(units, gcol)]
    gl = [a[dirs[d][7]:dirs[d][7] + 1, :] for (_, d, _), a in zip(units, gc)]
    decay = _each(lambda m, a, b: jnp.where(m, jnp.exp(jnp.where(m, a - b, 0.0)), 0.0), incl, gc, gc_row)
    kbeta = _each(lambda a, b: a * b, k, beta)
    k_bf = _each(lambda a: a.astype(BF16), k)
    lmat = _each(lambda m, a, b, dc: jnp.where(m, _dot_nt(a.astype(BF16), b) * dc, 0.0), strict, kbeta, k_bf, decay)
    a_qk = _each(lambda m, a, b, dc: jnp.where(m, _dot_nt(a.astype(BF16), b) * dc, 0.0), incl, q, k_bf, decay)
    tp = _unit_lower_inverse_minus_identity(lmat, ri, ci)
    egc = _each(jnp.exp, gc)
    rhs = _each(lambda a, b, kb, e: jnp.concatenate([a * b, kb * e], axis=1), v, beta, kbeta, egc)
    uw = _each(lambda r, t: r + _dot(t.astype(BF16), r.astype(BF16)), rhs, tp)
    kdec_t = _each(lambda a, t, c: jnp.concatenate([a * jnp.exp(t - c), zpad], axis=0).T.astype(BF16), k, gl, gc)
    qdec = _each(lambda a, e: (a * e).astype(BF16), q, egc)
    s = [s_ref[bi, d * B_HEADS + h] for bi, d, h in units]
    s_bf = _each(lambda a: a.astype(BF16), s)
    vn_bf = _each(lambda a, b: (a[:, :B_HEAD_DIM] - _dot(a[:, B_HEAD_DIM:].astype(BF16), b)).astype(BF16), uw, s_bf)
    o = _each(lambda a, b, m, w: _dot(a, b) + _dot(m.astype(BF16), w), qdec, s_bf, a_qk, vn_bf)
    bpad = jnp.zeros((n, LANES), BF16)
    s_new = _each(lambda a, t, kt, w: a * jnp.exp(t) + _dot(kt, jnp.concatenate([w, bpad], axis=0)), s, gl, kdec_t, vn_bf)
    for (bi, d, h), o_u, s_u in zip(units, o, s_new):
        dirs[d][4][bi, h] = o_u.astype(BF16)
        s_ref[bi, d * B_HEADS + h] = s_u


def _gdn_scan(q, k, v, bg, s0, nseq, lseq):
    nc = lseq // CHUNK
    nb = SCAN_SEQS if nseq % SCAN_SEQS == 0 else 1
    bg3 = bg.reshape(nseq, lseq, LANES)
    hspec_f = pl.BlockSpec((nb, B_HEADS, CHUNK, LANES), lambda b, c: (b, 0, c, 0))
    hspec_b = pl.BlockSpec((nb, B_HEADS, CHUNK, LANES), lambda b, c: (b, 0, nc - 1 - c, 0))
    gspec_f = pl.BlockSpec((nb, CHUNK, LANES), lambda b, c: (b, c, 0))
    gspec_b = pl.BlockSpec((nb, CHUNK, LANES), lambda b, c: (b, nc - 1 - c, 0))
    sspec = pl.BlockSpec((nb, 2 * B_HEADS, B_HEAD_DIM, B_HEAD_DIM), lambda b, c: (b, 0, 0, 0))
    oshape = jax.ShapeDtypeStruct((nseq, B_HEADS, lseq, LANES), BF16)
    return pl.pallas_call(
        _gdn_scan_kernel, grid=(nseq // nb, nc),
        in_specs=[hspec_f, hspec_f, hspec_f, gspec_f, hspec_b, hspec_b, hspec_b, gspec_b, sspec],
        out_specs=[hspec_f, hspec_b, sspec],
        out_shape=[oshape, oshape, jax.ShapeDtypeStruct(s0.shape, F32)],
        compiler_params=_cparams(2, 32), name="gdn_scan",
    )(q, k, v, bg3, q, k, v, bg3, s0)


def _gla_level_mats():
    n = CHUNK
    mats = []
    s = n // 2
    while s >= 1:
        m = np.zeros((n, n), np.float32)
        for i in range(n):
            ref = (i // (2 * s)) * 2 * s + s
            if i >= ref:
                m[i, ref + 1:i + 1] = 1.0
            else:
                m[i, i + 1:ref + 1] = -1.0
        mats.append(m)
        s //= 2
    mats.append(np.tril(np.ones((n, n), np.float32)))
    mats.append(np.triu(np.ones((n, n), np.float32), 1))
    fwd = np.concatenate(mats, 0)
    bwd = np.concatenate([m[::-1, ::-1] for m in mats], 0)
    return fwd, bwd


N_LEVELS = int(math.log2(CHUNK))


def _gla_scan_kernel(qf_ref, kf_ref, vf_ref, lowf_ref, qb_ref, kb_ref, vb_ref, lowb_ref, w2_ref, gb_ref,
                     mf_ref, mb_ref, s0_ref, of_ref, ob_ref, s_ref):
    c = pl.program_id(1)

    @pl.when(c == 0)
    def _():
        s_ref[...] = s0_ref[...]

    n = CHUNK
    dk_all = C_HEADS * C_KEY_DIM
    ri = lax.broadcasted_iota(jnp.int32, (n, n), 0)
    ci = lax.broadcasted_iota(jnp.int32, (n, n), 1)
    dirs = ((qf_ref, kf_ref, vf_ref, lowf_ref, mf_ref, of_ref, n - 1), (qb_ref, kb_ref, vb_ref, lowb_ref, mb_ref, ob_ref, 0))

    nb = qf_ref.shape[0]
    levs = {}
    for bi in range(nb):
        for d in range(2):
            low_ref, m_ref = dirs[d][3], dirs[d][4]
            logit = jnp.dot(low_ref[bi], w2_ref[:, d * dk_all:(d + 1) * dk_all], precision=lax.Precision.HIGHEST,
                            preferred_element_type=F32) + gb_ref[:, d * dk_all:(d + 1) * dk_all]
            log_a = (jnp.minimum(logit, 0.0) - jnp.log1p(jnp.exp(-jnp.abs(logit)))) * (1.0 / C_GATE_TAU)
            levs[bi, d] = _dot_sel(m_ref[...], log_a, parts=2)

    units = [(bi, d, h) for h in range(C_HEADS) for d in range(2) for bi in range(nb)]

    def lev(blk):
        return [levs[bi, d][blk * n:(blk + 1) * n, h * C_KEY_DIM:(h + 1) * C_KEY_DIM] for bi, d, h in units]

    q = [dirs[d][0][bi, h].astype(F32) for bi, d, h in units]
    k = [dirs[d][1][bi, h].astype(F32) for bi, d, h in units]
    v_bf = [dirs[d][2][bi, h] for bi, d, h in units]
    a = _each(lambda x, y: jnp.where(ri == ci, _dot_nt(x.astype(BF16), y.astype(BF16)), 0.0), q, k)
    for lvl in range(N_LEVELS):
        sh = N_LEVELS - 1 - lvl
        same = (ri >> (sh + 1)) == (ci >> (sh + 1))
        hi_r, hi_c = ((ri >> sh) & 1) == 1, ((ci >> sh) & 1) == 1
        pair = (same & hi_r & ~hi_c, same & ~hi_r & hi_c)
        x = _each(lambda t: jnp.exp(-jnp.abs(t)), lev(lvl))
        a = [acc + jnp.where(pair[d], _dot_nt((qq * xx).astype(BF16), (kk * xx).astype(BF16)), 0.0)
             for (_, d, _), acc, qq, kk, xx in zip(units, a, q, k, x)]
    bcum = lev(N_LEVELS)
    rest = lev(N_LEVELS + 1)
    st = [s_ref[bi, d * C_HEADS + h] for bi, d, h in units]
    o = _each(lambda qq, b, s, aa, vv: _dot_nt((qq * jnp.exp(b)).astype(BF16), s.astype(BF16)) + _dot(aa.astype(BF16), vv),
              q, bcum, st, a, v_bf)
    kpad = jnp.zeros((n, C_KEY_DIM), BF16)
    vpad = jnp.zeros((n, C_VAL_DIM), F32)
    kdec = _each(lambda kk, r: jnp.concatenate([(kk * jnp.exp(r)).astype(BF16), kpad], axis=0), k, rest)
    v_t = _each(lambda vv: jnp.concatenate([vv.astype(F32), vpad], axis=0).T.astype(BF16), v_bf)
    s_new = [s * jnp.exp(b[dirs[d][6]:dirs[d][6] + 1, :]) + _dot(vt, kd)
             for (_, d, _), s, b, vt, kd in zip(units, st, bcum, v_t, kdec)]
    for (bi, d, h), o_u, s_u in zip(units, o, s_new):
        dirs[d][5][bi, h] = o_u.astype(BF16)
        s_ref[bi, d * C_HEADS + h] = s_u


def _gla_scan(q, k, v, low, w2, gate_b, mats, s0, nseq, lseq):
    nc = lseq // CHUNK
    nb = SCAN_SEQS if nseq % SCAN_SEQS == 0 else 1
    low3 = low.reshape(nseq, lseq, LANES)
    mf, mb = mats

    def hspec(width, rev):
        return pl.BlockSpec((nb, C_HEADS, CHUNK, width), (lambda b, c: (b, 0, nc - 1 - c, 0)) if rev else (lambda b, c: (b, 0, c, 0)))

    def lspec(rev):
        return pl.BlockSpec((nb, CHUNK, LANES), (lambda b, c: (b, nc - 1 - c, 0)) if rev else (lambda b, c: (b, c, 0)))

    def cspec(shape):
        nd = len(shape)
        return pl.BlockSpec(shape, lambda b, c: (0,) * nd)

    sspec = pl.BlockSpec((nb, 2 * C_HEADS, C_VAL_DIM, C_KEY_DIM), lambda b, c: (b, 0, 0, 0))
    oshape = jax.ShapeDtypeStruct((nseq, C_HEADS, lseq, C_VAL_DIM), BF16)
    return pl.pallas_call(
        _gla_scan_kernel, grid=(nseq // nb, nc),
        in_specs=[hspec(C_KEY_DIM, False), hspec(C_KEY_DIM, False), hspec(C_VAL_DIM, False), lspec(False),
                  hspec(C_KEY_DIM, True), hspec(C_KEY_DIM, True), hspec(C_VAL_DIM, True), lspec(True),
                  cspec(w2.shape), cspec(gate_b.shape), cspec(mf.shape), cspec(mb.shape), sspec],
        out_specs=[hspec(C_VAL_DIM, False), hspec(C_VAL_DIM, True), sspec],
        out_shape=[oshape, oshape, jax.ShapeDtypeStruct(s0.shape, F32)],
        compiler_params=_cparams(2, 32), name="gla_scan",
    )(q, k, v, low3, q, k, v, low3, w2, gate_b, mf, mb, s0)


def _mix_ffn_kernel(*refs, heads, head_dim, tps):
    if heads:
        ofm, ofp, ofn, obm, obp, obn, gm, gp, gn, nw_ref = refs[:10]
        rest = refs[10:]
    else:
        om_ref, op_ref, on_ref = refs[:3]
        rest = refs[3:]
    (wo_ref, x_ref, xp_ref, xn_ref, mod_ref, g0_ref, b0_ref, wu_ref, cw_ref, cb_ref, wd_ref, g1_ref, b1_ref,
     o_ref, inp_ref, xe_ref, h_ref, act_ref) = rest
    t = pl.program_id(0) % tps
    tm = x_ref.shape[0]
    pad = SUBLANES
    main, before, after = slice(pad, pad + tm), slice(0, pad), slice(pad + tm, pad + tm + pad)

    if heads:
        def gated(of_ref, ob_ref, gate_ref):
            parts = []
            for h in range(heads):
                o = of_ref[0, h].astype(F32) + ob_ref[0, h].astype(F32)
                o = o * lax.rsqrt(jnp.mean(o * o, -1, keepdims=True) + NORM_EPS) * nw_ref[...]
                gate = gate_ref[:, h * head_dim:(h + 1) * head_dim].astype(F32)
                parts.append((o * _silu(gate)).astype(BF16))
            return jnp.concatenate(parts, axis=1)

        inp_ref[main, :] = gated(ofm, obm, gm)
        inp_ref[before, :] = gated(ofp, obp, gp)
        inp_ref[after, :] = gated(ofn, obn, gn)
    else:
        inp_ref[main, :] = om_ref[0].astype(F32).T.astype(BF16)
        inp_ref[before, :] = op_ref[0].astype(F32).T[LANES - pad:, :].astype(BF16)
        inp_ref[after, :] = on_ref[0].astype(F32).T[:pad, :].astype(BF16)
    xe_ref[main, :] = x_ref[...]
    xe_ref[before, :] = xp_ref[...]
    xe_ref[after, :] = xn_ref[...]

    y = _dot(inp_ref[...], wo_ref[...])
    xm = _layer_norm(DN_ALPHA * xe_ref[...] + mod_ref[0, 2:3, :] * y, g0_ref[...], b0_ref[...])
    xe_ref[...] = xm
    hm = _modulated(xm, mod_ref, 3)
    h_ref[main, :] = hm[main].astype(BF16)
    h_ref[before, :] = (hm[before] * (t > 0).astype(F32)).astype(BF16)
    h_ref[after, :] = (hm[after] * (t < tps - 1).astype(F32)).astype(BF16)
    for c0 in range(0, D_FF, FF_CHUNK):
        val = _dot(h_ref[pad:pad + tm, :], wu_ref[:, c0:c0 + FF_CHUNK])
        gx = _dot(h_ref[...], wu_ref[:, D_FF + c0:D_FF + c0 + FF_CHUNK])
        gt = (cw_ref[0:1, c0:c0 + FF_CHUNK] * pltpu.roll(gx, 1, 0)[pad:pad + tm]
              + cw_ref[1:2, c0:c0 + FF_CHUNK] * gx[pad:pad + tm]
              + cw_ref[2:3, c0:c0 + FF_CHUNK] * pltpu.roll(gx, tm + 2 * pad - 1, 0)[pad:pad + tm]
              + cb_ref[:, c0:c0 + FF_CHUNK])
        act_ref[:, c0:c0 + FF_CHUNK] = (_silu(gt) * val).astype(BF16)
    y = _dot(act_ref[...], wd_ref[...])
    o_ref[...] = _layer_norm(DN_ALPHA * xe_ref[main, :] + mod_ref[0, 5:6, :] * y, g1_ref[...], b1_ref[...])


def _mix_ffn(st, w_o, ln_g, ln_b, w_up, conv_w, conv_b, w_down, inp=None, scan_out=None, gate=None, norm_w=None):
    tm, tps = st.tm, st.tps
    r8 = tm // SUBLANES
    nb8 = st.rows // SUBLANES
    sb8 = st.lseq // SUBLANES

    def rows_before(width):
        return pl.BlockSpec((SUBLANES, width), lambda i: (jnp.maximum(i * r8 - 1, 0), 0))

    def rows_after(width):
        return pl.BlockSpec((SUBLANES, width), lambda i: (jnp.minimum((i + 1) * r8, nb8 - 1), 0))

    if scan_out is None:
        heads = head_dim = 0
        feat = inp.shape[1]
        rl = tm // LANES
        nbl = st.lseq // LANES
        in_specs = [pl.BlockSpec((1, feat, tm), lambda i: (i // tps, 0, i % tps)),
                    pl.BlockSpec((1, feat, LANES), lambda i: (i // tps, 0, jnp.maximum((i % tps) * rl - 1, 0))),
                    pl.BlockSpec((1, feat, LANES), lambda i: (i // tps, 0, jnp.minimum((i % tps + 1) * rl, nbl - 1)))]
        args = [inp, inp, inp]
    else:
        o_f, o_b = scan_out
        heads, head_dim = o_f.shape[1], o_f.shape[3]
        hb = pl.BlockSpec((1, heads, SUBLANES, head_dim), lambda i: (i // tps, 0, jnp.maximum((i % tps) * r8 - 1, 0), 0))
        ha = pl.BlockSpec((1, heads, SUBLANES, head_dim),
                          lambda i: (i // tps, 0, jnp.minimum((i % tps + 1) * r8, sb8 - 1), 0))
        hm = st.head_spec(heads, head_dim)
        width = heads * head_dim
        in_specs = [hm, hb, ha, hm, hb, ha, st.row_spec(width), rows_before(width), rows_after(width),
                    _const_spec((1, head_dim))]
        args = [o_f, o_f, o_f, o_b, o_b, o_b, gate, gate, gate, norm_w.reshape(1, -1)]
    in_specs += [_const_spec(w_o.shape), st.row_spec(D_MODEL), rows_before(D_MODEL), rows_after(D_MODEL), st.mod_spec(),
                 _const_spec((1, D_MODEL)), _const_spec((1, D_MODEL)), _const_spec(w_up.shape), _const_spec(conv_w.shape),
                 _const_spec((1, D_FF)), _const_spec(w_down.shape), _const_spec((1, D_MODEL)), _const_spec((1, D_MODEL))]
    args += [w_o, st.x, st.x, st.x, st.mod, ln_g[0].reshape(1, -1), ln_b[0].reshape(1, -1), w_up, conv_w,
             conv_b.reshape(1, -1), w_down, ln_g[1].reshape(1, -1), ln_b[1].reshape(1, -1)]
    ext = tm + 2 * SUBLANES
    return pl.pallas_call(
        functools.partial(_mix_ffn_kernel, heads=heads, head_dim=head_dim, tps=tps), grid=st.grid,
        in_specs=in_specs, out_specs=st.row_spec(D_MODEL),
        out_shape=jax.ShapeDtypeStruct((st.rows, D_MODEL), F32),
        scratch_shapes=[pltpu.VMEM((ext, w_o.shape[0]), BF16), pltpu.VMEM((ext, D_MODEL), F32),
                        pltpu.VMEM((ext, D_MODEL), BF16), pltpu.VMEM((tm, D_FF), BF16)],
        compiler_params=_cparams(1, 56), name="mix_ffn",
    )(*args)


def _rope_tables(n):
    rows = n // GRID_W
    r = jnp.repeat(jnp.arange(rows), GRID_W).astype(F32)
    col = jnp.tile(jnp.arange(GRID_W), rows).astype(F32)
    n_freq = A_HEAD_DIM // 4
    inv = jnp.power(ROPE_BASE, -jnp.arange(n_freq, dtype=F32) / n_freq)
    ang = jnp.concatenate([r[:, None] * inv, col[:, None] * inv], -1)
    cos, sin = jnp.cos(ang), jnp.sin(ang)
    zero = jnp.zeros_like(sin)
    reps = LANES // A_HEAD_DIM
    c = jnp.tile(jnp.concatenate([cos, cos], -1), (1, reps))
    s1 = jnp.tile(jnp.concatenate([-sin, zero], -1), (1, reps))
    s2 = jnp.tile(jnp.concatenate([zero, sin], -1), (1, reps))
    return c, s1, s2


def _pad_cols(w, width):
    return jnp.pad(w, ((0, 0), (0, width - w.shape[1])))


def kernel(x, c, ctx, c_ctx, ada_w, ada_b, ln_g, ln_b, ffn_w_up, ffn_conv_w, ffn_conv_b, ffn_w_down, attn_w_qkv, attn_sink, attn_w_o, gdn_w_in, gdn_conv_w, gdn_a_log, gdn_dt_bias, gdn_norm_w, gdn_w_o, gla_w_in, gla_w_gate2, gla_gate_b, gla_norm_w, gla_w_o):
    bsz, n, d = x.shape
    nctx = ctx.shape[1]
    assert d == D_MODEL and n % A_BLOCK == 0 and n % GRID_W == 0 and nctx % CHUNK == 0 and n % CHUNK == 0

    cond_rows = -(-(bsz + 1) // SUBLANES) * SUBLANES
    cond = jnp.zeros((cond_rows, d), F32).at[:bsz].set(c).at[bsz].set(c_ctx)
    mod_all = _ada(cond, ada_w, ada_b)

    rope_tabs = _rope_tables(n)
    gla_mats = tuple(jnp.asarray(m, BF16) for m in _gla_level_mats())

    xl = x.reshape(bsz * n, d)
    xc = ctx.reshape(bsz * nctx, d)
    for i in range(DEPTH):
        need_ctx_out = i < DEPTH - 1
        lat = _Stream(xl, mod_all[i, :bsz].reshape(bsz, 6, d), bsz, n)
        cx = _Stream(xc, mod_all[i, bsz:bsz + 1].reshape(1, 6, d), bsz, nctx)
        kind, slot = i % N_MIXERS, i // N_MIXERS
        if kind == 0:
            w_qkv = attn_w_qkv[slot].astype(BF16)
            w_o = attn_w_o[slot].astype(BF16)
            qt_c, k_c, vt_c = _attn_proj(cx, w_qkv, None)
            qt_l, k_l, vt_l = _attn_proj(lat, w_qkv, rope_tabs)
            mix_l = dict(inp=_attention(qt_l, k_l, vt_l, k_c, vt_c, attn_sink[slot], bsz, n, nctx, True))
            if need_ctx_out:
                mix_c = dict(inp=_attention(qt_c, None, None, k_c, vt_c, attn_sink[slot], bsz, nctx, nctx, False))
        elif kind == 1:
            hk = B_HEADS * B_HEAD_DIM
            w_in = gdn_w_in[slot]
            w_main = w_in[:, :4 * hk].astype(BF16)
            w_small = _pad_cols(w_in[:, 4 * hk:], LANES).astype(BF16)
            w_o = gdn_w_o[slot].astype(BF16)
            a_vec = jnp.zeros((1, LANES), F32).at[0, 2 * B_HEADS:4 * B_HEADS].set(jnp.exp(gdn_a_log[slot]).reshape(-1))
            dt_vec = jnp.zeros((1, LANES), F32).at[0, 2 * B_HEADS:4 * B_HEADS].set(gdn_dt_bias[slot].reshape(-1))
            state = jnp.zeros((bsz, 2 * B_HEADS, B_HEAD_DIM, B_HEAD_DIM), F32)
            outs = []
            for st in (cx, lat):
                q, k, v, gate, bg = _gdn_proj(st, w_main, w_small, gdn_conv_w[slot], a_vec, dt_vec)
                o_f, o_b, state = _gdn_scan(q, k, v, bg, state, st.nseq, st.lseq)
                outs.append(dict(scan_out=(o_f, o_b), gate=gate, norm_w=gdn_norm_w[slot]))
            mix_c, mix_l = outs
        else:
            dk, dv = C_HEADS * C_KEY_DIM, C_HEADS * C_VAL_DIM
            w_in = gla_w_in[slot]
            w_main = w_in[:, :2 * dk + 2 * dv].astype(BF16)
            w_small = _pad_cols(w_in[:, 2 * dk + 2 * dv:], LANES).astype(BF16)
            w_o = gla_w_o[slot].astype(BF16)
            w2 = jnp.zeros((LANES, 2 * dk), F32)
            for z in range(2):
                w2 = w2.at[z * C_GATE_RANK:(z + 1) * C_GATE_RANK, z * dk:(z + 1) * dk].set(gla_w_gate2[slot, z])
            gate_b = gla_gate_b[slot].reshape(1, 2 * dk)
            state = jnp.zeros((bsz, 2 * C_HEADS, C_VAL_DIM, C_KEY_DIM), F32)
            outs = []
            for st in (cx, lat):
                q, k, v, gate, low = _gla_proj(st, w_main, w_small)
                o_f, o_b, state = _gla_scan(q, k, v, low, w2, gate_b, gla_mats, state, st.nseq, st.lseq)
                outs.append(dict(scan_out=(o_f, o_b), gate=gate, norm_w=gla_norm_w[slot]))
            mix_c, mix_l = outs

        w_up = ffn_w_up[i].astype(BF16)
        w_down = ffn_w_down[i].astype(BF16)
        streams = [(lat, mix_l)] + ([(cx, mix_c)] if need_ctx_out else [])
        new = []
        for st, mix in streams:
            new.append(_mix_ffn(st, w_o, ln_g[i], ln_b[i], w_up, ffn_conv_w[i], ffn_conv_b[i], w_down, **mix))
        xl = new[0]
        if need_ctx_out:
            xc = new[1]
    return xl.reshape(bsz, n, d)
```

```python
import functools
import math

import numpy as np
import jax
import jax.numpy as jnp
from jax import lax
from jax.experimental import pallas as pl
from jax.experimental.pallas import tpu as pltpu

F32 = jnp.float32
BF16 = jnp.bfloat16

D_MODEL = 1024
DEPTH = 4
GRID_W = 64
N_MIXERS = 3
D_FF = 2816
A_HEADS = 16
A_KV_HEADS = 4
A_HEAD_DIM = 64
A_WINDOW = 128
A_BLOCK = 128
ROPE_BASE = 10000.0
B_HEADS = 8
B_HEAD_DIM = 128
C_HEADS = 4
C_KEY_DIM = 128
C_VAL_DIM = 256
C_GATE_RANK = 16
C_GATE_TAU = 16.0
CHUNK = 64
NORM_EPS = 1e-5
DN_ALPHA = (2 * DEPTH) ** 0.25

LANES = 128
SUBLANES = 8
V7X_VMEM_BYTES = 64 * 1024 * 1024
MIB = 1024 * 1024

ROW_TILE = 512
FF_CHUNK = 256
SCAN_SEQS = 2
NEG_BIG = -1e30
LOG2E = math.log2(math.e)


def _cparams(n_axes, vmem_mib):
    return pltpu.CompilerParams(dimension_semantics=("arbitrary",) * n_axes,
                                vmem_limit_bytes=min(vmem_mib * MIB, V7X_VMEM_BYTES - 8 * MIB))


def _row_tile(lseq, target=ROW_TILE):
    t = min(lseq, target)
    while lseq % t:
        t -= SUBLANES
    return t


def _dot(a, b):
    return jnp.dot(a, b, preferred_element_type=F32)


def _dot_nt(a, b):
    return lax.dot_general(a, b, (((1,), (1,)), ((), ())), preferred_element_type=F32)


def _split3(x):
    hi = x.astype(BF16)
    r = x - hi.astype(F32)
    mid = r.astype(BF16)
    lo = (r - mid.astype(F32)).astype(BF16)
    return hi, mid, lo


def _dot_sel(m_bf, x, parts=3):
    pieces = _split3(x)[:parts]
    acc = _dot(m_bf, pieces[0])
    for piece in pieces[1:]:
        acc = acc + _dot(m_bf, piece)
    return acc


def _silu(x):
    return x * jax.nn.sigmoid(x)


def _softplus(x):
    return jnp.maximum(x, 0.0) + jnp.log1p(jnp.exp(-jnp.abs(x)))


def _layer_norm(t, g, b):
    mu = jnp.mean(t, -1, keepdims=True)
    d = t - mu
    var = jnp.mean(d * d, -1, keepdims=True)
    return d * lax.rsqrt(var + NORM_EPS) * g + b


def _modulated(x, mod_ref, shift_row):
    sh = mod_ref[0, shift_row:shift_row + 1, :]
    sc = mod_ref[0, shift_row + 1:shift_row + 2, :]
    return x * (1.0 + sc) + sh


def _ada_kernel(s_ref, w_ref, b_ref, o_ref):
    s = _silu(s_ref[...])
    o_ref[0] = jnp.dot(s, w_ref[0], precision=lax.Precision.HIGHEST, preferred_element_type=F32) + b_ref[0]


def _ada(cond, ada_w, ada_b):
    depth, d, n = ada_w.shape
    rows = cond.shape[0]
    tn = 1536
    return pl.pallas_call(
        _ada_kernel,
        grid=(depth, n // tn),
        in_specs=[pl.BlockSpec((rows, d), lambda i, j: (0, 0)),
                  pl.BlockSpec((1, d, tn), lambda i, j: (i, 0, j)),
                  pl.BlockSpec((1, 1, tn), lambda i, j: (i, 0, j))],
        out_specs=pl.BlockSpec((1, rows, tn), lambda i, j: (i, 0, j)),
        out_shape=jax.ShapeDtypeStruct((depth, rows, n), F32),
        compiler_params=_cparams(2, 40),
        name="ada",
    )(cond, ada_w, ada_b.reshape(depth, 1, n))


class _Stream:
    def __init__(self, x, mod, nseq, lseq):
        self.x, self.mod, self.nseq, self.lseq = x, mod, nseq, lseq
        self.tm = _row_tile(lseq)
        self.tps = lseq // self.tm
        self.shared_mod = mod.shape[0] == 1

    def mod_spec(self):
        d = self.mod.shape[-1]
        if self.shared_mod:
            return pl.BlockSpec((1, 6, d), lambda i: (0, 0, 0))
        tps = self.tps
        return pl.BlockSpec((1, 6, d), lambda i: (i // tps, 0, 0))

    def row_spec(self, width, col_block=0):
        return pl.BlockSpec((self.tm, width), lambda i: (i, col_block))

    def head_spec(self, heads, width):
        tps = self.tps
        return pl.BlockSpec((1, heads, self.tm, width), lambda i: (i // tps, 0, i % tps, 0))

    def head_shape(self, heads, width, dtype):
        return jax.ShapeDtypeStruct((self.nseq, heads, self.lseq, width), dtype)

    @property
    def rows(self):
        return self.nseq * self.lseq

    @property
    def grid(self):
        return (self.rows // self.tm,)


def _const_spec(shape):
    nd = len(shape)
    return pl.BlockSpec(shape, lambda i: (0,) * nd, pipeline_mode=pl.Buffered(1))


def _attn_proj_kernel(x_ref, mod_ref, w_ref, *refs, rope):
    qt_ref, k_ref, vt_ref = refs[-3:]
    u = _modulated(x_ref[...], mod_ref, 0).astype(BF16)
    z = _dot(u, w_ref[...])
    nq = A_HEADS * A_HEAD_DIM // LANES
    nk = A_KV_HEADS * A_HEAD_DIM // LANES
    for j in range(z.shape[1] // LANES):
        t = z[:, j * LANES:(j + 1) * LANES]
        if j < nq + nk and rope:
            c_ref, s1_ref, s2_ref = refs[:3]
            t = (t * c_ref[...] + pltpu.roll(t, LANES - A_HEAD_DIM // 2, 1) * s1_ref[...]
                 + pltpu.roll(t, A_HEAD_DIM // 2, 1) * s2_ref[...])
        if j < nq:
            qt_ref[0, j * LANES:(j + 1) * LANES, :] = (t * (A_HEAD_DIM ** -0.5 * LOG2E)).T.astype(BF16)
        elif j < nq + nk:
            k_ref[:, (j - nq) * LANES:(j - nq + 1) * LANES] = t.astype(BF16)
        else:
            vt_ref[0, (j - nq - nk) * LANES:(j - nq - nk + 1) * LANES, :] = t.T.astype(BF16)


def _attn_proj(st, w_bf, rope_tabs):
    nq = A_HEADS * A_HEAD_DIM
    nkv = A_KV_HEADS * A_HEAD_DIM
    tps = st.tps
    in_specs = [st.row_spec(D_MODEL), st.mod_spec(), _const_spec(w_bf.shape)]
    args = [st.x, st.mod, w_bf]
    if rope_tabs is not None:
        for t in rope_tabs:
            in_specs.append(pl.BlockSpec((st.tm, LANES), lambda i: (i % tps, 0)))
            args.append(t)

    def fm_spec(feat):
        return pl.BlockSpec((1, feat, st.tm), lambda i: (i // tps, 0, i % tps))

    return pl.pallas_call(
        functools.partial(_attn_proj_kernel, rope=rope_tabs is not None),
        grid=st.grid, in_specs=in_specs, out_specs=[fm_spec(nq), st.row_spec(nkv), fm_spec(nkv)],
        out_shape=[jax.ShapeDtypeStruct((st.nseq, nq, st.lseq), BF16), jax.ShapeDtypeStruct((st.rows, nkv), BF16),
                   jax.ShapeDtypeStruct((st.nseq, nkv, st.lseq), BF16)],
        compiler_params=_cparams(1, 40), name="attn_proj",
    )(*args)


def _gdn_proj_kernel(x_ref, xp_ref, xn_ref, mod_ref, w_ref, ws_ref, cw_ref, av_ref, dt_ref,
                     q_ref, k_ref, v_ref, gate_ref, bg_ref, u_ref, *, tps):
    t = pl.program_id(0) % tps
    tm = x_ref.shape[0]
    pad = SUBLANES
    hk = B_HEADS * B_HEAD_DIM
    u_ref[pad:pad + tm, :] = _modulated(x_ref[...], mod_ref, 0).astype(BF16)
    u_ref[0:pad, :] = (_modulated(xp_ref[...], mod_ref, 0) * (t > 0).astype(F32)).astype(BF16)
    u_ref[pad + tm:, :] = (_modulated(xn_ref[...], mod_ref, 0) * (t < tps - 1).astype(F32)).astype(BF16)
    width = 2 * LANES
    for c0 in range(0, 3 * hk, width):
        zx = _dot(u_ref[...], w_ref[:, c0:c0 + width])
        y = _silu(cw_ref[0:1, c0:c0 + width] * pltpu.roll(zx, 1, 0)[pad:pad + tm]
                  + cw_ref[1:2, c0:c0 + width] * zx[pad:pad + tm]
                  + cw_ref[2:3, c0:c0 + width] * pltpu.roll(zx, tm + 2 * pad - 1, 0)[pad:pad + tm])
        for j in range(c0 // LANES, (c0 + width) // LANES):
            yj = y[:, j * LANES - c0:(j + 1) * LANES - c0]
            if j < 2 * B_HEADS:
                yj = yj * lax.rsqrt(jnp.sum(yj * yj, -1, keepdims=True) + 1e-6)
            if j < B_HEADS:
                q_ref[0, j] = (yj * (B_HEAD_DIM ** -0.5)).astype(BF16)
            elif j < 2 * B_HEADS:
                k_ref[0, j - B_HEADS] = yj.astype(BF16)
            else:
                v_ref[0, j - 2 * B_HEADS] = yj.astype(BF16)
    u = u_ref[pad:pad + tm, :]
    gate_ref[...] = _dot(u, w_ref[:, 3 * hk:]).astype(BF16)
    zs = _dot(u, ws_ref[...])
    lane = lax.broadcasted_iota(jnp.int32, zs.shape, 1)
    bg_ref[...] = jnp.where(lane < 2 * B_HEADS, jax.nn.sigmoid(zs), -av_ref[...] * _softplus(zs + dt_ref[...]))


def _gdn_proj(st, w_main, w_small, conv_w, a_vec, dt_vec):
    hk = B_HEADS * B_HEAD_DIM
    tm, tps = st.tm, st.tps
    r8 = tm // SUBLANES
    nb8 = st.rows // SUBLANES
    hspec = st.head_spec(B_HEADS, LANES)
    hshape = st.head_shape(B_HEADS, LANES, BF16)
    return pl.pallas_call(
        functools.partial(_gdn_proj_kernel, tps=tps), grid=st.grid,
        in_specs=[st.row_spec(D_MODEL),
                  pl.BlockSpec((SUBLANES, D_MODEL), lambda i: (jnp.maximum(i * r8 - 1, 0), 0)),
                  pl.BlockSpec((SUBLANES, D_MODEL), lambda i: (jnp.minimum((i + 1) * r8, nb8 - 1), 0)),
                  st.mod_spec(), _const_spec(w_main.shape), _const_spec(w_small.shape), _const_spec(conv_w.shape),
                  _const_spec(a_vec.shape), _const_spec(dt_vec.shape)],
        out_specs=[hspec, hspec, hspec, st.row_spec(hk), st.row_spec(LANES)],
        out_shape=[hshape, hshape, hshape, jax.ShapeDtypeStruct((st.rows, hk), BF16),
                   jax.ShapeDtypeStruct((st.rows, LANES), F32)],
        scratch_shapes=[pltpu.VMEM((tm + 2 * SUBLANES, D_MODEL), BF16)],
        compiler_params=_cparams(1, 48), name="gdn_proj",
    )(st.x, st.x, st.x, st.mod, w_main, w_small, conv_w, a_vec, dt_vec)


def _gla_proj_kernel(x_ref, mod_ref, w_ref, ws_ref, q_ref, k_ref, v_ref, gate_ref, low_ref):
    u = _modulated(x_ref[...], mod_ref, 0).astype(BF16)
    z = _dot(u, w_ref[...])
    dk, dv = C_HEADS * C_KEY_DIM, C_HEADS * C_VAL_DIM
    for h in range(C_HEADS):
        q_ref[0, h] = (z[:, h * C_KEY_DIM:(h + 1) * C_KEY_DIM] * (C_KEY_DIM ** -0.5)).astype(BF16)
        k_ref[0, h] = z[:, dk + h * C_KEY_DIM:dk + (h + 1) * C_KEY_DIM].astype(BF16)
        v_ref[0, h] = z[:, 2 * dk + h * C_VAL_DIM:2 * dk + (h + 1) * C_VAL_DIM].astype(BF16)
    gate_ref[...] = z[:, 2 * dk + dv:].astype(BF16)
    low_ref[...] = _dot(u, ws_ref[...])


def _gla_proj(st, w_main, w_small):
    dv = C_HEADS * C_VAL_DIM
    return pl.pallas_call(
        _gla_proj_kernel, grid=st.grid,
        in_specs=[st.row_spec(D_MODEL), st.mod_spec(), _const_spec(w_main.shape), _const_spec(w_small.shape)],
        out_specs=[st.head_spec(C_HEADS, C_KEY_DIM), st.head_spec(C_HEADS, C_KEY_DIM),
                   st.head_spec(C_HEADS, C_VAL_DIM), st.row_spec(dv), st.row_spec(LANES)],
        out_shape=[st.head_shape(C_HEADS, C_KEY_DIM, BF16), st.head_shape(C_HEADS, C_KEY_DIM, BF16),
                   st.head_shape(C_HEADS, C_VAL_DIM, BF16), jax.ShapeDtypeStruct((st.rows, dv), BF16),
                   jax.ShapeDtypeStruct((st.rows, LANES), F32)],
        compiler_params=_cparams(1, 48), name="gla_proj",
    )(st.x, st.mod, w_main, w_small)


def _attn_kernel(sink_ref, qt_ref, *refs, local, nblk):
    if local:
        kp_ref, kc_ref, kn_ref, vp_ref, vc_ref, vn_ref, kx_ref, vx_ref, o_ref = refs
    else:
        kx_ref, vx_ref, o_ref = refs
    j = pl.program_id(1)
    grp = A_HEADS // A_KV_HEADS
    hd = A_HEAD_DIM
    blk = qt_ref.shape[2]
    nctx = kx_ref.shape[0]
    if local:
        kr = lax.broadcasted_iota(jnp.int32, (blk, grp * blk), 0)
        qi = lax.broadcasted_iota(jnp.int32, (blk, grp * blk), 1) & (blk - 1)
        bias_prev = jnp.where((kr >= qi) & (j > 0), 0.0, NEG_BIG)
        bias_next = jnp.where((kr <= qi) & (j < nblk - 1), 0.0, NEG_BIG)

        def masked(s):
            return jnp.concatenate([s[:blk] + bias_prev, s[blk:2 * blk], s[2 * blk:3 * blk] + bias_next, s[3 * blk:]], axis=0)
    heads = list(range(A_KV_HEADS))
    hs = [slice(h * hd, (h + 1) * hd) for h in heads]
    qt = [jnp.concatenate([qt_ref[0, (h * grp + g) * hd:(h * grp + g + 1) * hd, :] for g in range(grp)], axis=1)
          for h in heads]
    snk = [jnp.concatenate([jnp.full((1, blk), sink_ref[h * grp + g] * LOG2E, F32) for g in range(grp)], axis=1)
           for h in heads]
    if local:
        k_all = [jnp.concatenate([kp_ref[:, s], kc_ref[:, s], kn_ref[:, s], kx_ref[:, s]], axis=0) for s in hs]
        vt_all = [jnp.concatenate([vp_ref[0, s, :], vc_ref[0, s, :], vn_ref[0, s, :], vx_ref[0, s, :]], axis=1) for s in hs]
        s_t = _each(lambda a, b: masked(_dot(a, b)), k_all, qt)
    else:
        k_all = [kx_ref[:, s] for s in hs]
        vt_all = [vx_ref[0, s, :] for s in hs]
        s_t = _each(_dot, k_all, qt)
    m = _each(lambda s, k: jnp.maximum(jnp.max(s, 0, keepdims=True), k), s_t, snk)
    p = _each(lambda s, a: jnp.exp2(s - a), s_t, m)
    den = _each(lambda pp, k, a: jnp.sum(pp, 0, keepdims=True) + jnp.exp2(k - a), p, snk, m)
    acc = _each(lambda v, pp: _dot(v, pp.astype(BF16)), vt_all, p)
    for h, a, dn in zip(heads, acc, den):
        o = a / dn
        for g in range(grp):
            r0 = (h * grp + g) * hd
            o_ref[0, r0:r0 + hd, :] = o[:, g * blk:(g + 1) * blk].astype(BF16)


def _attention(qt, k, vt, k_ctx, vt_ctx, sink, nseq, lseq, lctx, local):
    blk = A_BLOCK
    nblk = lseq // blk
    nq = A_HEADS * A_HEAD_DIM
    nkv = A_KV_HEADS * A_HEAD_DIM
    nbrs = (lambda j: jnp.maximum(j - 1, 0), lambda j: j, lambda j: jnp.minimum(j + 1, nblk - 1))

    in_specs = [pl.BlockSpec(memory_space=pltpu.SMEM), pl.BlockSpec((1, nq, blk), lambda b, j: (b, 0, j))]
    args = [sink, qt]
    if local:
        for f in nbrs:
            in_specs.append(pl.BlockSpec((blk, nkv), (lambda f: lambda b, j: (b * nblk + f(j), 0))(f)))
            args.append(k)
        for f in nbrs:
            in_specs.append(pl.BlockSpec((1, nkv, blk), (lambda f: lambda b, j: (b, 0, f(j)))(f)))
            args.append(vt)
    in_specs += [pl.BlockSpec((lctx, nkv), lambda b, j: (b, 0)), pl.BlockSpec((1, nkv, lctx), lambda b, j: (b, 0, 0))]
    args += [k_ctx, vt_ctx]
    return pl.pallas_call(
        functools.partial(_attn_kernel, local=local, nblk=nblk),
        grid=(nseq, nblk), in_specs=in_specs,
        out_specs=pl.BlockSpec((1, nq, blk), lambda b, j: (b, 0, j)),
        out_shape=jax.ShapeDtypeStruct((nseq, nq, lseq), BF16),
        compiler_params=_cparams(2, 32), name="attn_local" if local else "attn_ctx",
    )(*args)


def _each(f, *lists):
    return [f(*a) for a in zip(*lists)]


def _unit_lower_inverse_minus_identity(lmats, ri, ci):
    b16 = (ri >> 4) == (ci >> 4)
    b32 = (ri >> 5) == (ci >> 5)

    def mm(a, b):
        return _dot(a.astype(BF16), b.astype(BF16))

    l_bd = _each(lambda l: jnp.where(b16, l, 0.0), lmats)
    m2 = _each(mm, l_bd, l_bd)
    m4 = _each(mm, m2, m2)
    p = _each(lambda l, a: a - l - mm(l, a), l_bd, m2)
    m8 = _each(mm, m4, m4)
    p = _each(lambda x, a: x + a + mm(x, a), p, m4)
    p = _each(lambda x, a: x + a + mm(x, a), p, m8)
    for sel in (lambda l: jnp.where(b32 & ~b16, l, 0.0), lambda l: jnp.where(b32, 0.0, l)):
        lo = _each(sel, lmats)
        y = _each(lambda x, a: a + mm(x, a), p, lo)
        p = _each(lambda x, a: x - (a + mm(a, x)), p, y)
    return p


def _gdn_scan_kernel(qf_ref, kf_ref, vf_ref, bgf_ref, qb_ref, kb_ref, vb_ref, bgb_ref, s0_ref,
                     of_ref, ob_ref, s_ref):
    c = pl.program_id(1)

    @pl.when(c == 0)
    def _():
        s_ref[...] = s0_ref[...]

    n = CHUNK
    ri = lax.broadcasted_iota(jnp.int32, (n, n), 0)
    ci = lax.broadcasted_iota(jnp.int32, (n, n), 1)
    zpad = jnp.zeros((n, LANES), F32)
    dirs = ((qf_ref, kf_ref, vf_ref, bgf_ref, of_ref, ri >= ci, ri > ci, n - 1),
            (qb_ref, kb_ref, vb_ref, bgb_ref, ob_ref, ri <= ci, ri < ci, 0))
    nb = qf_ref.shape[0]
    cums = {}
    for bi in range(nb):
        for d in range(2):
            bg = dirs[d][3][bi]
            gc_all = _dot_sel(dirs[d][5].astype(BF16), bg)
            cums[bi, d] = (bg, gc_all, jnp.concatenate([gc_all, zpad], axis=0).T)

    units = [(bi, d, h) for h in range(B_HEADS) for d in range(2) for bi in range(nb)]
    incl = [dirs[d][5] for _, d, _ in units]
    strict = [dirs[d][6] for _, d, _ in units]
    q = [dirs[d][0][bi, h].astype(F32) for bi, d, h in units]
    k = [dirs[d][1][bi, h].astype(F32) for bi, d, h in units]
    v = [dirs[d][2][bi, h].astype(F32) for bi, d, h in units]
    beta = [cums[bi, d][0][:, d * B_HEADS + h:d * B_HEADS + h + 1] for bi, d, h in units]
    gcol = [(2 + d) * B_HEADS + h for _, d, h in units]
    gc = [cums[bi, d][1][:, g:g + 1] for (bi, d, _), g in zip(units, gcol)]
    gc_row = [cums[bi, d][2][g:g + 1, :n] for (bi, d, _), g in zip(units, gcol)]
    gl = [a[dirs[d][7]:dirs[d][7] + 1, :] for (_, d, _), a in zip(units, gc)]
    decay = _each(lambda m, a, b: jnp.where(m, jnp.exp(jnp.where(m, a - b, 0.0)), 0.0), incl, gc, gc_row)
    kbeta = _each(lambda a, b: a * b, k, beta)
    k_bf = _each(lambda a: a.astype(BF16), k)
    lmat = _each(lambda m, a, b, dc: jnp.where(m, _dot_nt(a.astype(BF16), b) * dc, 0.0), strict, kbeta, k_bf, decay)
    a_qk = _each(lambda m, a, b, dc: jnp.where(m, _dot_nt(a.astype(BF16), b) * dc, 0.0), incl, q, k_bf, decay)
    tp = _unit_lower_inverse_minus_identity(lmat, ri, ci)
    egc = _each(jnp.exp, gc)
    rhs = _each(lambda a, b, kb, e: jnp.concatenate([a * b, kb * e], axis=1), v, beta, kbeta, egc)
    uw = _each(lambda r, t: r + _dot(t.astype(BF16), r.astype(BF16)), rhs, tp)
    kdec_t = _each(lambda a, t, c: jnp.concatenate([a * jnp.exp(t - c), zpad], axis=0).T.astype(BF16), k, gl, gc)
    qdec = _each(lambda a, e: (a * e).astype(BF16), q, egc)
    s = [s_ref[bi, d * B_HEADS + h] for bi, d, h in units]
    s_bf = _each(lambda a: a.astype(BF16), s)
    vn_bf = _each(lambda a, b: (a[:, :B_HEAD_DIM] - _dot(a[:, B_HEAD_DIM:].astype(BF16), b)).astype(BF16), uw, s_bf)
    o = _each(lambda a, b, m, w: _dot(a, b) + _dot(m.astype(BF16), w), qdec, s_bf, a_qk, vn_bf)
    bpad = jnp.zeros((n, LANES), BF16)
    s_new = _each(lambda a, t, kt, w: a * jnp.exp(t) + _dot(kt, jnp.concatenate([w, bpad], axis=0)), s, gl, kdec_t, vn_bf)
    for (bi, d, h), o_u, s_u in zip(units, o, s_new):
        dirs[d][4][bi, h] = o_u.astype(BF16)
        s_ref[bi, d * B_HEADS + h] = s_u


def _gdn_scan(q, k, v, bg, s0, nseq, lseq):
    nc = lseq // CHUNK
    nb = SCAN_SEQS if nseq % SCAN_SEQS == 0 else 1
    bg3 = bg.reshape(nseq, lseq, LANES)
    hspec_f = pl.BlockSpec((nb, B_HEADS, CHUNK, LANES), lambda b, c: (b, 0, c, 0))
    hspec_b = pl.BlockSpec((nb, B_HEADS, CHUNK, LANES), lambda b, c: (b, 0, nc - 1 - c, 0))
    gspec_f = pl.BlockSpec((nb, CHUNK, LANES), lambda b, c: (b, c, 0))
    gspec_b = pl.BlockSpec((nb, CHUNK, LANES), lambda b, c: (b, nc - 1 - c, 0))
    sspec = pl.BlockSpec((nb, 2 * B_HEADS, B_HEAD_DIM, B_HEAD_DIM), lambda b, c: (b, 0, 0, 0))
    oshape = jax.ShapeDtypeStruct((nseq, B_HEADS, lseq, LANES), BF16)
    return pl.pallas_call(
        _gdn_scan_kernel, grid=(nseq // nb, nc),
        in_specs=[hspec_f, hspec_f, hspec_f, gspec_f, hspec_b, hspec_b, hspec_b, gspec_b, sspec],
        out_specs=[hspec_f, hspec_b, sspec],
        out_shape=[oshape, oshape, jax.ShapeDtypeStruct(s0.shape, F32)],
        compiler_params=_cparams(2, 32), name="gdn_scan",
    )(q, k, v, bg3, q, k, v, bg3, s0)


def _gla_level_mats():
    n = CHUNK
    mats = []
    s = n // 2
    while s >= 1:
        m = np.zeros((n, n), np.float32)
        for i in range(n):
            ref = (i // (2 * s)) * 2 * s + s
            if i >= ref:
                m[i, ref + 1:i + 1] = 1.0
            else:
                m[i, i + 1:ref + 1] = -1.0
        mats.append(m)
        s //= 2
    mats.append(np.tril(np.ones((n, n), np.float32)))
    mats.append(np.triu(np.ones((n, n), np.float32), 1))
    fwd = np.concatenate(mats, 0)
    bwd = np.concatenate([m[::-1, ::-1] for m in mats], 0)
    return fwd, bwd


N_LEVELS = int(math.log2(CHUNK))


def _gla_scan_kernel(qf_ref, kf_ref, vf_ref, lowf_ref, qb_ref, kb_ref, vb_ref, lowb_ref, w2_ref, w2lo_ref, gb_ref,
                     mf_ref, mb_ref, s0_ref, of_ref, ob_ref, s_ref):
    c = pl.program_id(1)

    @pl.when(c == 0)
    def _():
        s_ref[...] = s0_ref[...]

    n = CHUNK
    dk_all = C_HEADS * C_KEY_DIM
    ri = lax.broadcasted_iota(jnp.int32, (n, n), 0)
    ci = lax.broadcasted_iota(jnp.int32, (n, n), 1)
    dirs = ((qf_ref, kf_ref, vf_ref, lowf_ref, mf_ref, of_ref, n - 1), (qb_ref, kb_ref, vb_ref, lowb_ref, mb_ref, ob_ref, 0))

    nb = qf_ref.shape[0]
    levs = {}
    for bi in range(nb):
        for d in range(2):
            low_ref, m_ref = dirs[d][3], dirs[d][4]
            cols = slice(d * dk_all, (d + 1) * dk_all)
            low = low_ref[bi]
            low_hi = low.astype(BF16)
            low_lo = (low - low_hi.astype(F32)).astype(BF16)
            logit = (_dot(low_hi, w2_ref[:, cols]) + _dot(low_hi, w2lo_ref[:, cols]) + _dot(low_lo, w2_ref[:, cols])
                     + gb_ref[:, cols])
            log_a = (jnp.minimum(logit, 0.0) - jnp.log1p(jnp.exp(-jnp.abs(logit)))) * (1.0 / C_GATE_TAU)
            levs[bi, d] = _dot_sel(m_ref[...], log_a, parts=2)

    units = [(bi, d, h) for h in range(C_HEADS) for d in range(2) for bi in range(nb)]

    def lev(blk):
        return [levs[bi, d][blk * n:(blk + 1) * n, h * C_KEY_DIM:(h + 1) * C_KEY_DIM] for bi, d, h in units]

    q = [dirs[d][0][bi, h].astype(F32) for bi, d, h in units]
    k = [dirs[d][1][bi, h].astype(F32) for bi, d, h in units]
    v_bf = [dirs[d][2][bi, h] for bi, d, h in units]
    a = _each(lambda x, y: jnp.where(ri == ci, _dot_nt(x.astype(BF16), y.astype(BF16)), 0.0), q, k)
    for lvl in range(N_LEVELS):
        sh = N_LEVELS - 1 - lvl
        same = (ri >> (sh + 1)) == (ci >> (sh + 1))
        hi_r, hi_c = ((ri >> sh) & 1) == 1, ((ci >> sh) & 1) == 1
        pair = (same & hi_r & ~hi_c, same & ~hi_r & hi_c)
        x = _each(lambda t: jnp.exp(-jnp.abs(t)), lev(lvl))
        a = [acc + jnp.where(pair[d], _dot_nt((qq * xx).astype(BF16), (kk * xx).astype(BF16)), 0.0)
             for (_, d, _), acc, qq, kk, xx in zip(units, a, q, k, x)]
    bcum = lev(N_LEVELS)
    rest = lev(N_LEVELS + 1)
    st = [s_ref[bi, d * C_HEADS + h] for bi, d, h in units]
    o = _each(lambda qq, b, s, aa, vv: _dot_nt((qq * jnp.exp(b)).astype(BF16), s.astype(BF16)) + _dot(aa.astype(BF16), vv),
              q, bcum, st, a, v_bf)
    kpad = jnp.zeros((n, C_KEY_DIM), BF16)
    vpad = jnp.zeros((n, C_VAL_DIM), F32)
    kdec = _each(lambda kk, r: jnp.concatenate([(kk * jnp.exp(r)).astype(BF16), kpad], axis=0), k, rest)
    v_t = _each(lambda vv: jnp.concatenate([vv.astype(F32), vpad], axis=0).T.astype(BF16), v_bf)
    s_new = [s * jnp.exp(b[dirs[d][6]:dirs[d][6] + 1, :]) + _dot(vt, kd)
             for (_, d, _), s, b, vt, kd in zip(units, st, bcum, v_t, kdec)]
    for (bi, d, h), o_u, s_u in zip(units, o, s_new):
        dirs[d][5][bi, h] = o_u.astype(BF16)
        s_ref[bi, d * C_HEADS + h] = s_u


def _gla_scan(q, k, v, low, w2, gate_b, mats, s0, nseq, lseq):
    nc = lseq // CHUNK
    nb = SCAN_SEQS if nseq % SCAN_SEQS == 0 else 1
    low3 = low.reshape(nseq, lseq, LANES)
    mf, mb = mats
    w2_hi = w2.astype(BF16)
    w2_lo = (w2 - w2_hi.astype(F32)).astype(BF16)

    def hspec(width, rev):
        return pl.BlockSpec((nb, C_HEADS, CHUNK, width), (lambda b, c: (b, 0, nc - 1 - c, 0)) if rev else (lambda b, c: (b, 0, c, 0)))

    def lspec(rev):
        return pl.BlockSpec((nb, CHUNK, LANES), (lambda b, c: (b, nc - 1 - c, 0)) if rev else (lambda b, c: (b, c, 0)))

    def cspec(shape):
        nd = len(shape)
        return pl.BlockSpec(shape, lambda b, c: (0,) * nd)

    sspec = pl.BlockSpec((nb, 2 * C_HEADS, C_VAL_DIM, C_KEY_DIM), lambda b, c: (b, 0, 0, 0))
    oshape = jax.ShapeDtypeStruct((nseq, C_HEADS, lseq, C_VAL_DIM), BF16)
    return pl.pallas_call(
        _gla_scan_kernel, grid=(nseq // nb, nc),
        in_specs=[hspec(C_KEY_DIM, False), hspec(C_KEY_DIM, False), hspec(C_VAL_DIM, False), lspec(False),
                  hspec(C_KEY_DIM, True), hspec(C_KEY_DIM, True), hspec(C_VAL_DIM, True), lspec(True),
                  cspec(w2.shape), cspec(w2.shape), cspec(gate_b.shape), cspec(mf.shape), cspec(mb.shape), sspec],
        out_specs=[hspec(C_VAL_DIM, False), hspec(C_VAL_DIM, True), sspec],
        out_shape=[oshape, oshape, jax.ShapeDtypeStruct(s0.shape, F32)],
        compiler_params=_cparams(2, 32), name="gla_scan",
    )(q, k, v, low3, q, k, v, low3, w2_hi, w2_lo, gate_b, mf, mb, s0)


def _mix_ffn_kernel(*refs, heads, head_dim, tps):
    if heads:
        ofm, ofp, ofn, obm, obp, obn, gm, gp, gn, nw_ref = refs[:10]
        rest = refs[10:]
    else:
        om_ref, op_ref, on_ref = refs[:3]
        rest = refs[3:]
    (wo_ref, x_ref, xp_ref, xn_ref, mod_ref, g0_ref, b0_ref, wu_ref, cw_ref, cb_ref, wd_ref, g1_ref, b1_ref,
     o_ref, inp_ref, xe_ref, h_ref, act_ref) = rest
    t = pl.program_id(0) % tps
    tm = x_ref.shape[0]
    pad = SUBLANES
    main, before, after = slice(pad, pad + tm), slice(0, pad), slice(pad + tm, pad + tm + pad)

    if heads:
        def gated(of_ref, ob_ref, gate_ref):
            parts = []
            for h in range(heads):
                o = of_ref[0, h].astype(F32) + ob_ref[0, h].astype(F32)
                o = o * lax.rsqrt(jnp.mean(o * o, -1, keepdims=True) + NORM_EPS) * nw_ref[...]
                gate = gate_ref[:, h * head_dim:(h + 1) * head_dim].astype(F32)
                parts.append((o * _silu(gate)).astype(BF16))
            return jnp.concatenate(parts, axis=1)

        inp_ref[main, :] = gated(ofm, obm, gm)
        inp_ref[before, :] = gated(ofp, obp, gp)
        inp_ref[after, :] = gated(ofn, obn, gn)
    else:
        inp_ref[main, :] = om_ref[0].astype(F32).T.astype(BF16)
        inp_ref[before, :] = op_ref[0].astype(F32).T[LANES - pad:, :].astype(BF16)
        inp_ref[after, :] = on_ref[0].astype(F32).T[:pad, :].astype(BF16)
    xe_ref[main, :] = x_ref[...]
    xe_ref[before, :] = xp_ref[...]
    xe_ref[after, :] = xn_ref[...]

    y = _dot(inp_ref[...], wo_ref[...])
    xm = _layer_norm(DN_ALPHA * xe_ref[...] + mod_ref[0, 2:3, :] * y, g0_ref[...], b0_ref[...])
    xe_ref[...] = xm
    hm = _modulated(xm, mod_ref, 3)
    h_ref[main, :] = hm[main].astype(BF16)
    h_ref[before, :] = (hm[before] * (t > 0).astype(F32)).astype(BF16)
    h_ref[after, :] = (hm[after] * (t < tps - 1).astype(F32)).astype(BF16)
    for c0 in range(0, D_FF, FF_CHUNK):
        val = _dot(h_ref[pad:pad + tm, :], wu_ref[:, c0:c0 + FF_CHUNK])
        gx = _dot(h_ref[...], wu_ref[:, D_FF + c0:D_FF + c0 + FF_CHUNK])
        gt = (cw_ref[0:1, c0:c0 + FF_CHUNK] * pltpu.roll(gx, 1, 0)[pad:pad + tm]
              + cw_ref[1:2, c0:c0 + FF_CHUNK] * gx[pad:pad + tm]
              + cw_ref[2:3, c0:c0 + FF_CHUNK] * pltpu.roll(gx, tm + 2 * pad - 1, 0)[pad:pad + tm]
              + cb_ref[:, c0:c0 + FF_CHUNK])
        act_ref[:, c0:c0 + FF_CHUNK] = (_silu(gt) * val).astype(BF16)
    y = _dot(act_ref[...], wd_ref[...])
    o_ref[...] = _layer_norm(DN_ALPHA * xe_ref[main, :] + mod_ref[0, 5:6, :] * y, g1_ref[...], b1_ref[...])


def _mix_ffn(st, w_o, ln_g, ln_b, w_up, conv_w, conv_b, w_down, inp=None, scan_out=None, gate=None, norm_w=None):
    tm, tps = st.tm, st.tps
    r8 = tm // SUBLANES
    nb8 = st.rows // SUBLANES
    sb8 = st.lseq // SUBLANES

    def rows_before(width):
        return pl.BlockSpec((SUBLANES, width), lambda i: (jnp.maximum(i * r8 - 1, 0), 0))

    def rows_after(width):
        return pl.BlockSpec((SUBLANES, width), lambda i: (jnp.minimum((i + 1) * r8, nb8 - 1), 0))

    if scan_out is None:
        heads = head_dim = 0
        feat = inp.shape[1]
        rl = tm // LANES
        nbl = st.lseq // LANES
        in_specs = [pl.BlockSpec((1, feat, tm), lambda i: (i // tps, 0, i % tps)),
                    pl.BlockSpec((1, feat, LANES), lambda i: (i // tps, 0, jnp.maximum((i % tps) * rl - 1, 0))),
                    pl.BlockSpec((1, feat, LANES), lambda i: (i // tps, 0, jnp.minimum((i % tps + 1) * rl, nbl - 1)))]
        args = [inp, inp, inp]
    else:
        o_f, o_b = scan_out
        heads, head_dim = o_f.shape[1], o_f.shape[3]
        hb = pl.BlockSpec((1, heads, SUBLANES, head_dim), lambda i: (i // tps, 0, jnp.maximum((i % tps) * r8 - 1, 0), 0))
        ha = pl.BlockSpec((1, heads, SUBLANES, head_dim),
                          lambda i: (i // tps, 0, jnp.minimum((i % tps + 1) * r8, sb8 - 1), 0))
        hm = st.head_spec(heads, head_dim)
        width = heads * head_dim
        in_specs = [hm, hb, ha, hm, hb, ha, st.row_spec(width), rows_before(width), rows_after(width),
                    _const_spec((1, head_dim))]
        args = [o_f, o_f, o_f, o_b, o_b, o_b, gate, gate, gate, norm_w.reshape(1, -1)]
    in_specs += [_const_spec(w_o.shape), st.row_spec(D_MODEL), rows_before(D_MODEL), rows_after(D_MODEL), st.mod_spec(),
                 _const_spec((1, D_MODEL)), _const_spec((1, D_MODEL)), _const_spec(w_up.shape), _const_spec(conv_w.shape),
                 _const_spec((1, D_FF)), _const_spec(w_down.shape), _const_spec((1, D_MODEL)), _const_spec((1, D_MODEL))]
    args += [w_o, st.x, st.x, st.x, st.mod, ln_g[0].reshape(1, -1), ln_b[0].reshape(1, -1), w_up, conv_w,
             conv_b.reshape(1, -1), w_down, ln_g[1].reshape(1, -1), ln_b[1].reshape(1, -1)]
    ext = tm + 2 * SUBLANES
    return pl.pallas_call(
        functools.partial(_mix_ffn_kernel, heads=heads, head_dim=head_dim, tps=tps), grid=st.grid,
        in_specs=in_specs, out_specs=st.row_spec(D_MODEL),
        out_shape=jax.ShapeDtypeStruct((st.rows, D_MODEL), F32),
        scratch_shapes=[pltpu.VMEM((ext, w_o.shape[0]), BF16), pltpu.VMEM((ext, D_MODEL), F32),
                        pltpu.VMEM((ext, D_MODEL), BF16), pltpu.VMEM((tm, D_FF), BF16)],
        compiler_params=_cparams(1, 56), name="mix_ffn",
    )(*args)


def _rope_tables(n):
    rows = n // GRID_W
    r = jnp.repeat(jnp.arange(rows), GRID_W).astype(F32)
    col = jnp.tile(jnp.arange(GRID_W), rows).astype(F32)
    n_freq = A_HEAD_DIM // 4
    inv = jnp.power(ROPE_BASE, -jnp.arange(n_freq, dtype=F32) / n_freq)
    ang = jnp.concatenate([r[:, None] * inv, col[:, None] * inv], -1)
    cos, sin = jnp.cos(ang), jnp.sin(ang)
    zero = jnp.zeros_like(sin)
    reps = LANES // A_HEAD_DIM
    c = jnp.tile(jnp.concatenate([cos, cos], -1), (1, reps))
    s1 = jnp.tile(jnp.concatenate([-sin, zero], -1), (1, reps))
    s2 = jnp.tile(jnp.concatenate([zero, sin], -1), (1, reps))
    return c, s1, s2


def _pad_cols(w, width):
    return jnp.pad(w, ((0, 0), (0, width - w.shape[1])))


def kernel(x, c, ctx, c_ctx, ada_w, ada_b, ln_g, ln_b, ffn_w_up, ffn_conv_w, ffn_conv_b, ffn_w_down, attn_w_qkv, attn_sink, attn_w_o, gdn_w_in, gdn_conv_w, gdn_a_log, gdn_dt_bias, gdn_norm_w, gdn_w_o, gla_w_in, gla_w_gate2, gla_gate_b, gla_norm_w, gla_w_o):
    bsz, n, d = x.shape
    nctx = ctx.shape[1]
    assert d == D_MODEL and n % A_BLOCK == 0 and n % GRID_W == 0 and nctx % CHUNK == 0 and n % CHUNK == 0
    assert A_WINDOW == A_BLOCK and nctx % A_BLOCK == 0

    cond_rows = -(-(bsz + 1) // SUBLANES) * SUBLANES
    cond = jnp.zeros((cond_rows, d), F32).at[:bsz].set(c).at[bsz].set(c_ctx)
    mod_all = _ada(cond, ada_w, ada_b)

    rope_tabs = _rope_tables(n)
    gla_mats = tuple(jnp.asarray(m, BF16) for m in _gla_level_mats())

    xl = x.reshape(bsz * n, d)
    xc = ctx.reshape(bsz * nctx, d)
    for i in range(DEPTH):
        need_ctx_out = i < DEPTH - 1
        lat = _Stream(xl, mod_all[i, :bsz].reshape(bsz, 6, d), bsz, n)
        cx = _Stream(xc, mod_all[i, bsz:bsz + 1].reshape(1, 6, d), bsz, nctx)
        kind, slot = i % N_MIXERS, i // N_MIXERS
        if kind == 0:
            w_qkv = attn_w_qkv[slot].astype(BF16)
            w_o = attn_w_o[slot].astype(BF16)
            qt_c, k_c, vt_c = _attn_proj(cx, w_qkv, None)
            qt_l, k_l, vt_l = _attn_proj(lat, w_qkv, rope_tabs)
            mix_l = dict(inp=_attention(qt_l, k_l, vt_l, k_c, vt_c, attn_sink[slot], bsz, n, nctx, True))
            if need_ctx_out:
                mix_c = dict(inp=_attention(qt_c, None, None, k_c, vt_c, attn_sink[slot], bsz, nctx, nctx, False))
        elif kind == 1:
            hk = B_HEADS * B_HEAD_DIM
            w_in = gdn_w_in[slot]
            w_main = w_in[:, :4 * hk].astype(BF16)
            w_small = _pad_cols(w_in[:, 4 * hk:], LANES).astype(BF16)
            w_o = gdn_w_o[slot].astype(BF16)
            a_vec = jnp.zeros((1, LANES), F32).at[0, 2 * B_HEADS:4 * B_HEADS].set(jnp.exp(gdn_a_log[slot]).reshape(-1))
            dt_vec = jnp.zeros((1, LANES), F32).at[0, 2 * B_HEADS:4 * B_HEADS].set(gdn_dt_bias[slot].reshape(-1))
            state = jnp.zeros((bsz, 2 * B_HEADS, B_HEAD_DIM, B_HEAD_DIM), F32)
            outs = []
            for st in (cx, lat):
                q, k, v, gate, bg = _gdn_proj(st, w_main, w_small, gdn_conv_w[slot], a_vec, dt_vec)
                o_f, o_b, state = _gdn_scan(q, k, v, bg, state, st.nseq, st.lseq)
                outs.append(dict(scan_out=(o_f, o_b), gate=gate, norm_w=gdn_norm_w[slot]))
            mix_c, mix_l = outs
        else:
            dk, dv = C_HEADS * C_KEY_DIM, C_HEADS * C_VAL_DIM
            w_in = gla_w_in[slot]
            w_main = w_in[:, :2 * dk + 2 * dv].astype(BF16)
            w_small = _pad_cols(w_in[:, 2 * dk + 2 * dv:], LANES).astype(BF16)
            w_o = gla_w_o[slot].astype(BF16)
            w2 = jnp.zeros((LANES, 2 * dk), F32)
            for z in range(2):
                w2 = w2.at[z * C_GATE_RANK:(z + 1) * C_GATE_RANK, z * dk:(z + 1) * dk].set(gla_w_gate2[slot, z])
            gate_b = gla_gate_b[slot].reshape(1, 2 * dk)
            state = jnp.zeros((bsz, 2 * C_HEADS, C_VAL_DIM, C_KEY_DIM), F32)
            outs = []
            for st in (cx, lat):
                q, k, v, gate, low = _gla_proj(st, w_main, w_small)
                o_f, o_b, state = _gla_scan(q, k, v, low, w2, gate_b, gla_mats, state, st.nseq, st.lseq)
                outs.append(dict(scan_out=(o_f, o_b), gate=gate, norm_w=gla_norm_w[slot]))
            mix_c, mix_l = outs

        w_up = ffn_w_up[i].astype(BF16)
        w_down = ffn_w_down[i].astype(BF16)
        streams = [(lat, mix_l)] + ([(cx, mix_c)] if need_ctx_out else [])
        new = []
        for st, mix in streams:
            new.append(_mix_ffn(st, w_o, ln_g[i], ln_b[i], w_up, ffn_conv_w[i], ffn_conv_b[i], w_down, **mix))
        xl = new[0]
        if need_ctx_out:
            xc = new[1]
    return xl.reshape(bsz, n, d)
```

```python
import functools
import math

import numpy as np
import jax
import jax.numpy as jnp
from jax import lax
from jax.experimental import pallas as pl
from jax.experimental.pallas import tpu as pltpu

F32 = jnp.float32
BF16 = jnp.bfloat16

D_MODEL = 1024
DEPTH = 4
GRID_W = 64
N_MIXERS = 3
D_FF = 2816
A_HEADS = 16
A_KV_HEADS = 4
A_HEAD_DIM = 64
A_WINDOW = 128
A_BLOCK = 128
ROPE_BASE = 10000.0
B_HEADS = 8
B_HEAD_DIM = 128
C_HEADS = 4
C_KEY_DIM = 128
C_VAL_DIM = 256
C_GATE_RANK = 16
C_GATE_TAU = 16.0
CHUNK = 64
NORM_EPS = 1e-5
L2_EPS = 1e-6
DN_ALPHA = (2 * DEPTH) ** 0.25

LANES = 128
SUBLANES = 8
V7X_VMEM_BYTES = 64 * 1024 * 1024
MIB = 1024 * 1024

ROW_TILE = 512
FF_CHUNK = 512
ADA_COL_TILE = 1536
SCAN_SEQS = 2
NEG_BIG = -1e30
LOG2E = math.log2(math.e)


_VMEM_MIB = dict(ada=40, attn_proj=40, gdn_proj=48, gla_proj=48, attn=32, scan=32, mix_ffn=56)


def _cparams(n_axes, kernel):
    return pltpu.CompilerParams(dimension_semantics=("arbitrary",) * n_axes,
                                vmem_limit_bytes=min(_VMEM_MIB[kernel] * MIB, V7X_VMEM_BYTES - 8 * MIB))


def _row_tile(lseq, target=ROW_TILE):
    t = min(lseq, target)
    while lseq % t:
        t -= SUBLANES
    return t


def _dot(a, b):
    return jnp.dot(a, b, preferred_element_type=F32)


def _dot_nt(a, b):
    return lax.dot_general(a, b, (((1,), (1,)), ((), ())), preferred_element_type=F32)


def _split3(x):
    hi = x.astype(BF16)
    r = x - hi.astype(F32)
    mid = r.astype(BF16)
    lo = (r - mid.astype(F32)).astype(BF16)
    return hi, mid, lo


def _dot_sel(m_bf, x, parts=3):
    pieces = _split3(x)[:parts]
    acc = _dot(m_bf, pieces[0])
    for piece in pieces[1:]:
        acc = acc + _dot(m_bf, piece)
    return acc


def _silu(x):
    return x * jax.nn.sigmoid(x)


def _softplus(x):
    return jnp.maximum(x, 0.0) + jnp.log1p(jnp.exp(-jnp.abs(x)))


def _layer_norm(t, g, b):
    mu = jnp.mean(t, -1, keepdims=True)
    d = t - mu
    var = jnp.mean(d * d, -1, keepdims=True)
    return d * lax.rsqrt(var + NORM_EPS) * g + b


def _modulated(x, mod_ref, shift_row):
    sh = mod_ref[0, shift_row:shift_row + 1, :]
    sc = mod_ref[0, shift_row + 1:shift_row + 2, :]
    return x * (1.0 + sc) + sh


def _ada_kernel(s_ref, w_ref, b_ref, o_ref):
    s = _silu(s_ref[...])
    o_ref[0] = jnp.dot(s, w_ref[0], precision=lax.Precision.HIGHEST, preferred_element_type=F32) + b_ref[0]


def _ada(cond, ada_w, ada_b):
    depth, d, n = ada_w.shape
    rows = cond.shape[0]
    tn = ADA_COL_TILE
    return pl.pallas_call(
        _ada_kernel,
        grid=(depth, n // tn),
        in_specs=[pl.BlockSpec((rows, d), lambda i, j: (0, 0)),
                  pl.BlockSpec((1, d, tn), lambda i, j: (i, 0, j)),
                  pl.BlockSpec((1, 1, tn), lambda i, j: (i, 0, j))],
        out_specs=pl.BlockSpec((1, rows, tn), lambda i, j: (i, 0, j)),
        out_shape=jax.ShapeDtypeStruct((depth, rows, n), F32),
        compiler_params=_cparams(2, "ada"),
        name="ada",
    )(cond, ada_w, ada_b.reshape(depth, 1, n))


class _Stream:
    def __init__(self, x, mod, nseq, lseq):
        self.x, self.mod, self.nseq, self.lseq = x, mod, nseq, lseq
        self.tm = _row_tile(lseq)
        self.tps = lseq // self.tm
        self.shared_mod = mod.shape[0] == 1

    def mod_spec(self):
        d = self.mod.shape[-1]
        if self.shared_mod:
            return pl.BlockSpec((1, 6, d), lambda i: (0, 0, 0))
        tps = self.tps
        return pl.BlockSpec((1, 6, d), lambda i: (i // tps, 0, 0))

    def row_spec(self, width, col_block=0):
        return pl.BlockSpec((self.tm, width), lambda i: (i, col_block))

    def head_spec(self, heads, width):
        tps = self.tps
        return pl.BlockSpec((1, heads, self.tm, width), lambda i: (i // tps, 0, i % tps, 0))

    def head_shape(self, heads, width, dtype):
        return jax.ShapeDtypeStruct((self.nseq, heads, self.lseq, width), dtype)

    @property
    def rows(self):
        return self.nseq * self.lseq

    @property
    def grid(self):
        return (self.rows // self.tm,)


def _const_spec(shape):
    nd = len(shape)
    return pl.BlockSpec(shape, lambda i: (0,) * nd, pipeline_mode=pl.Buffered(1))


def _attn_proj_kernel(x_ref, mod_ref, w_ref, *refs, rope):
    qt_ref, k_ref, vt_ref = refs[-3:]
    u = _modulated(x_ref[...], mod_ref, 0).astype(BF16)
    z = _dot(u, w_ref[...])
    nq = A_HEADS * A_HEAD_DIM // LANES
    nk = A_KV_HEADS * A_HEAD_DIM // LANES
    for j in range(z.shape[1] // LANES):
        t = z[:, j * LANES:(j + 1) * LANES]
        if j < nq + nk and rope:
            c_ref, s1_ref, s2_ref = refs[:3]
            t = (t * c_ref[...] + pltpu.roll(t, LANES - A_HEAD_DIM // 2, 1) * s1_ref[...]
                 + pltpu.roll(t, A_HEAD_DIM // 2, 1) * s2_ref[...])
        if j < nq:
            qt_ref[0, j * LANES:(j + 1) * LANES, :] = (t * (A_HEAD_DIM ** -0.5 * LOG2E)).T.astype(BF16)
        elif j < nq + nk:
            k_ref[:, (j - nq) * LANES:(j - nq + 1) * LANES] = t.astype(BF16)
        else:
            vt_ref[0, (j - nq - nk) * LANES:(j - nq - nk + 1) * LANES, :] = t.T.astype(BF16)


def _attn_proj(st, w_bf, rope_tabs):
    nq = A_HEADS * A_HEAD_DIM
    nkv = A_KV_HEADS * A_HEAD_DIM
    tps = st.tps
    in_specs = [st.row_spec(D_MODEL), st.mod_spec(), _const_spec(w_bf.shape)]
    args = [st.x, st.mod, w_bf]
    if rope_tabs is not None:
        for t in rope_tabs:
            in_specs.append(pl.BlockSpec((st.tm, LANES), lambda i: (i % tps, 0)))
            args.append(t)

    def fm_spec(feat):
        return pl.BlockSpec((1, feat, st.tm), lambda i: (i // tps, 0, i % tps))

    return pl.pallas_call(
        functools.partial(_attn_proj_kernel, rope=rope_tabs is not None),
        grid=st.grid, in_specs=in_specs, out_specs=[fm_spec(nq), st.row_spec(nkv), fm_spec(nkv)],
        out_shape=[jax.ShapeDtypeStruct((st.nseq, nq, st.lseq), BF16), jax.ShapeDtypeStruct((st.rows, nkv), BF16),
                   jax.ShapeDtypeStruct((st.nseq, nkv, st.lseq), BF16)],
        compiler_params=_cparams(1, "attn_proj"), name="attn_proj",
    )(*args)


def _gdn_proj_kernel(x_ref, xp_ref, xn_ref, mod_ref, w_ref, ws_ref, cw_ref, av_ref, dt_ref,
                     q_ref, k_ref, v_ref, gate_ref, bg_ref, u_ref, *, tps):
    t = pl.program_id(0) % tps
    tm = x_ref.shape[0]
    pad = SUBLANES
    hk = B_HEADS * B_HEAD_DIM
    u_ref[pad:pad + tm, :] = _modulated(x_ref[...], mod_ref, 0).astype(BF16)
    u_ref[0:pad, :] = (_modulated(xp_ref[...], mod_ref, 0) * (t > 0).astype(F32)).astype(BF16)
    u_ref[pad + tm:, :] = (_modulated(xn_ref[...], mod_ref, 0) * (t < tps - 1).astype(F32)).astype(BF16)
    width = 2 * LANES
    for c0 in range(0, 3 * hk, width):
        zx = _dot(u_ref[...], w_ref[:, c0:c0 + width])
        y = _silu(cw_ref[0:1, c0:c0 + width] * pltpu.roll(zx, 1, 0)[pad:pad + tm]
                  + cw_ref[1:2, c0:c0 + width] * zx[pad:pad + tm]
                  + cw_ref[2:3, c0:c0 + width] * pltpu.roll(zx, tm + 2 * pad - 1, 0)[pad:pad + tm])
        for j in range(c0 // LANES, (c0 + width) // LANES):
            yj = y[:, j * LANES - c0:(j + 1) * LANES - c0]
            if j < 2 * B_HEADS:
                yj = yj * lax.rsqrt(jnp.sum(yj * yj, -1, keepdims=True) + L2_EPS)
            if j < B_HEADS:
                q_ref[0, j] = (yj * (B_HEAD_DIM ** -0.5)).astype(BF16)
            elif j < 2 * B_HEADS:
                k_ref[0, j - B_HEADS] = yj.astype(BF16)
            else:
                v_ref[0, j - 2 * B_HEADS] = yj.astype(BF16)
    u = u_ref[pad:pad + tm, :]
    gate_ref[...] = _dot(u, w_ref[:, 3 * hk:]).astype(BF16)
    zs = _dot(u, ws_ref[...])
    lane = lax.broadcasted_iota(jnp.int32, zs.shape, 1)
    bg_ref[...] = jnp.where(lane < 2 * B_HEADS, jax.nn.sigmoid(zs), -av_ref[...] * _softplus(zs + dt_ref[...]))


def _gdn_proj(st, w_main, w_small, conv_w, a_vec, dt_vec):
    hk = B_HEADS * B_HEAD_DIM
    tm, tps = st.tm, st.tps
    r8 = tm // SUBLANES
    nb8 = st.rows // SUBLANES
    hspec = st.head_spec(B_HEADS, LANES)
    hshape = st.head_shape(B_HEADS, LANES, BF16)
    return pl.pallas_call(
        functools.partial(_gdn_proj_kernel, tps=tps), grid=st.grid,
        in_specs=[st.row_spec(D_MODEL),
                  pl.BlockSpec((SUBLANES, D_MODEL), lambda i: (jnp.maximum(i * r8 - 1, 0), 0)),
                  pl.BlockSpec((SUBLANES, D_MODEL), lambda i: (jnp.minimum((i + 1) * r8, nb8 - 1), 0)),
                  st.mod_spec(), _const_spec(w_main.shape), _const_spec(w_small.shape), _const_spec(conv_w.shape),
                  _const_spec(a_vec.shape), _const_spec(dt_vec.shape)],
        out_specs=[hspec, hspec, hspec, st.row_spec(hk), st.row_spec(LANES)],
        out_shape=[hshape, hshape, hshape, jax.ShapeDtypeStruct((st.rows, hk), BF16),
                   jax.ShapeDtypeStruct((st.rows, LANES), F32)],
        scratch_shapes=[pltpu.VMEM((tm + 2 * SUBLANES, D_MODEL), BF16)],
        compiler_params=_cparams(1, "gdn_proj"), name="gdn_proj",
    )(st.x, st.x, st.x, st.mod, w_main, w_small, conv_w, a_vec, dt_vec)


def _gla_proj_kernel(x_ref, mod_ref, w_ref, ws_ref, q_ref, k_ref, v_ref, gate_ref, low_ref):
    u = _modulated(x_ref[...], mod_ref, 0).astype(BF16)
    z = _dot(u, w_ref[...])
    dk, dv = C_HEADS * C_KEY_DIM, C_HEADS * C_VAL_DIM
    for h in range(C_HEADS):
        q_ref[0, h] = (z[:, h * C_KEY_DIM:(h + 1) * C_KEY_DIM] * (C_KEY_DIM ** -0.5)).astype(BF16)
        k_ref[0, h] = z[:, dk + h * C_KEY_DIM:dk + (h + 1) * C_KEY_DIM].astype(BF16)
        v_ref[0, h] = z[:, 2 * dk + h * C_VAL_DIM:2 * dk + (h + 1) * C_VAL_DIM].astype(BF16)
    gate_ref[...] = z[:, 2 * dk + dv:].astype(BF16)
    low_ref[...] = _dot(u, ws_ref[...])


def _gla_proj(st, w_main, w_small):
    dv = C_HEADS * C_VAL_DIM
    return pl.pallas_call(
        _gla_proj_kernel, grid=st.grid,
        in_specs=[st.row_spec(D_MODEL), st.mod_spec(), _const_spec(w_main.shape), _const_spec(w_small.shape)],
        out_specs=[st.head_spec(C_HEADS, C_KEY_DIM), st.head_spec(C_HEADS, C_KEY_DIM),
                   st.head_spec(C_HEADS, C_VAL_DIM), st.row_spec(dv), st.row_spec(LANES)],
        out_shape=[st.head_shape(C_HEADS, C_KEY_DIM, BF16), st.head_shape(C_HEADS, C_KEY_DIM, BF16),
                   st.head_shape(C_HEADS, C_VAL_DIM, BF16), jax.ShapeDtypeStruct((st.rows, dv), BF16),
                   jax.ShapeDtypeStruct((st.rows, LANES), F32)],
        compiler_params=_cparams(1, "gla_proj"), name="gla_proj",
    )(st.x, st.mod, w_main, w_small)


def _attn_kernel(sink_ref, qt_ref, *refs, local, nblk):
    if local:
        kp_ref, kc_ref, kn_ref, vp_ref, vc_ref, vn_ref, kx_ref, vx_ref, o_ref = refs
    else:
        kx_ref, vx_ref, o_ref = refs
    j = pl.program_id(1)
    grp = A_HEADS // A_KV_HEADS
    hd = A_HEAD_DIM
    blk = qt_ref.shape[2]
    nctx = kx_ref.shape[0]
    if local:
        kr = lax.broadcasted_iota(jnp.int32, (blk, grp * blk), 0)
        qi = lax.broadcasted_iota(jnp.int32, (blk, grp * blk), 1) & (blk - 1)
        bias_prev = jnp.where((kr >= qi) & (j > 0), 0.0, NEG_BIG)
        bias_next = jnp.where((kr <= qi) & (j < nblk - 1), 0.0, NEG_BIG)

        def masked(s):
            return jnp.concatenate([s[:blk] + bias_prev, s[blk:2 * blk], s[2 * blk:3 * blk] + bias_next, s[3 * blk:]], axis=0)
    heads = list(range(A_KV_HEADS))
    hs = [slice(h * hd, (h + 1) * hd) for h in heads]
    qt = [jnp.concatenate([qt_ref[0, (h * grp + g) * hd:(h * grp + g + 1) * hd, :] for g in range(grp)], axis=1)
          for h in heads]
    snk = [jnp.concatenate([jnp.full((1, blk), sink_ref[h * grp + g] * LOG2E, F32) for g in range(grp)], axis=1)
           for h in heads]
    if local:
        k_all = [jnp.concatenate([kp_ref[:, s], kc_ref[:, s], kn_ref[:, s], kx_ref[:, s]], axis=0) for s in hs]
        vt_all = [jnp.concatenate([vp_ref[0, s, :], vc_ref[0, s, :], vn_ref[0, s, :], vx_ref[0, s, :]], axis=1) for s in hs]
        s_t = _each(lambda a, b: masked(_dot(a, b)), k_all, qt)
    else:
        k_all = [kx_ref[:, s] for s in hs]
        vt_all = [vx_ref[0, s, :] for s in hs]
        s_t = _each(_dot, k_all, qt)
    m = _each(lambda s, k: jnp.maximum(jnp.max(s, 0, keepdims=True), k), s_t, snk)
    p = _each(lambda s, a: jnp.exp2(s - a), s_t, m)
    den = _each(lambda pp, k, a: jnp.sum(pp, 0, keepdims=True) + jnp.exp2(k - a), p, snk, m)
    acc = _each(lambda v, pp: _dot(v, pp.astype(BF16)), vt_all, p)
    for h, a, dn in zip(heads, acc, den):
        o = a / dn
        for g in range(grp):
            r0 = (h * grp + g) * hd
            o_ref[0, r0:r0 + hd, :] = o[:, g * blk:(g + 1) * blk].astype(BF16)


def _attention(qt, k, vt, k_ctx, vt_ctx, sink, nseq, lseq, lctx, local):
    blk = A_BLOCK
    nblk = lseq // blk
    nq = A_HEADS * A_HEAD_DIM
    nkv = A_KV_HEADS * A_HEAD_DIM
    nbrs = (lambda j: jnp.maximum(j - 1, 0), lambda j: j, lambda j: jnp.minimum(j + 1, nblk - 1))

    in_specs = [pl.BlockSpec(memory_space=pltpu.SMEM), pl.BlockSpec((1, nq, blk), lambda b, j: (b, 0, j))]
    args = [sink, qt]
    if local:
        for f in nbrs:
            in_specs.append(pl.BlockSpec((blk, nkv), (lambda f: lambda b, j: (b * nblk + f(j), 0))(f)))
            args.append(k)
        for f in nbrs:
            in_specs.append(pl.BlockSpec((1, nkv, blk), (lambda f: lambda b, j: (b, 0, f(j)))(f)))
            args.append(vt)
    in_specs += [pl.BlockSpec((lctx, nkv), lambda b, j: (b, 0)), pl.BlockSpec((1, nkv, lctx), lambda b, j: (b, 0, 0))]
    args += [k_ctx, vt_ctx]
    return pl.pallas_call(
        functools.partial(_attn_kernel, local=local, nblk=nblk),
        grid=(nseq, nblk), in_specs=in_specs,
        out_specs=pl.BlockSpec((1, nq, blk), lambda b, j: (b, 0, j)),
        out_shape=jax.ShapeDtypeStruct((nseq, nq, lseq), BF16),
        compiler_params=_cparams(2, "attn"), name="attn_local" if local else "attn_ctx",
    )(*args)


def _each(f, *lists):
    return [f(*a) for a in zip(*lists)]


def _unit_lower_inverse_minus_identity(lmats, ri, ci):
    b16 = (ri >> 4) == (ci >> 4)
    b32 = (ri >> 5) == (ci >> 5)

    def mm(a, b):
        return _dot(a.astype(BF16), b.astype(BF16))

    l_bd = _each(lambda l: jnp.where(b16, l, 0.0), lmats)
    m2 = _each(mm, l_bd, l_bd)
    m4 = _each(mm, m2, m2)
    p = _each(lambda l, a: a - l - mm(l, a), l_bd, m2)
    m8 = _each(mm, m4, m4)
    p = _each(lambda x, a: x + a + mm(x, a), p, m4)
    p = _each(lambda x, a: x + a + mm(x, a), p, m8)
    for sel in (lambda l: jnp.where(b32 & ~b16, l, 0.0), lambda l: jnp.where(b32, 0.0, l)):
        lo = _each(sel, lmats)
        y = _each(lambda x, a: a + mm(x, a), p, lo)
        p = _each(lambda x, a: x - (a + mm(a, x)), p, y)
    return p


def _gdn_scan_kernel(qf_ref, kf_ref, vf_ref, bgf_ref, qb_ref, kb_ref, vb_ref, bgb_ref, s0_ref,
                     of_ref, ob_ref, s_ref):
    c = pl.program_id(1)

    @pl.when(c == 0)
    def _():
        s_ref[...] = s0_ref[...]

    n = CHUNK
    ri = lax.broadcasted_iota(jnp.int32, (n, n), 0)
    ci = lax.broadcasted_iota(jnp.int32, (n, n), 1)
    zpad = jnp.zeros((n, LANES), F32)
    dirs = ((qf_ref, kf_ref, vf_ref, bgf_ref, of_ref, ri >= ci, ri > ci, n - 1),
            (qb_ref, kb_ref, vb_ref, bgb_ref, ob_ref, ri <= ci, ri < ci, 0))
    nb = qf_ref.shape[0]
    cums = {}
    for bi in range(nb):
        for d in range(2):
            bg = dirs[d][3][bi]
            gc_all = _dot_sel(dirs[d][5].astype(BF16), bg)
            cums[bi, d] = (bg, gc_all, jnp.concatenate([gc_all, zpad], axis=0).T)

    units = [(bi, d, h) for h in range(B_HEADS) for d in range(2) for bi in range(nb)]
    incl = [dirs[d][5] for _, d, _ in units]
    strict = [dirs[d][6] for _, d, _ in units]
    q = [dirs[d][0][bi, h].astype(F32) for bi, d, h in units]
    k = [dirs[d][1][bi, h].astype(F32) for bi, d, h in units]
    v = [dirs[d][2][bi, h].astype(F32) for bi, d, h in units]
    beta = [cums[bi, d][0][:, d * B_HEADS + h:d * B_HEADS + h + 1] for bi, d, h in units]
    gcol = [(2 + d) * B_HEADS + h for _, d, h in units]
    gc = [cums[bi, d][1][:, g:g + 1] for (bi, d, _), g in zip(units, gcol)]
    gc_row = [cums[bi, d][2][g:g + 1, :n] for (bi, d, _), g in zip(units, gcol)]
    gl = [a[dirs[d][7]:dirs[d][7] + 1, :] for (_, d, _), a in zip(units, gc)]
    decay = _each(lambda m, a, b: jnp.where(m, jnp.exp(jnp.where(m, a - b, 0.0)), 0.0), incl, gc, gc_row)
    kbeta = _each(lambda a, b: a * b, k, beta)
    k_bf = _each(lambda a: a.astype(BF16), k)
    lmat = _each(lambda m, a, b, dc: jnp.where(m, _dot_nt(a.astype(BF16), b) * dc, 0.0), strict, kbeta, k_bf, decay)
    a_qk = _each(lambda m, a, b, dc: jnp.where(m, _dot_nt(a.astype(BF16), b) * dc, 0.0), incl, q, k_bf, decay)
    tp = _unit_lower_inverse_minus_identity(lmat, ri, ci)
    egc = _each(jnp.exp, gc)
    rhs = _each(lambda a, b, kb, e: jnp.concatenate([a * b, kb * e], axis=1), v, beta, kbeta, egc)
    uw = _each(lambda r, t: r + _dot(t.astype(BF16), r.astype(BF16)), rhs, tp)
    kdec_t = _each(lambda a, t, c: jnp.concatenate([a * jnp.exp(t - c), zpad], axis=0).T.astype(BF16), k, gl, gc)
    qdec = _each(lambda a, e: (a * e).astype(BF16), q, egc)
    s = [s_ref[bi, d * B_HEADS + h] for bi, d, h in units]
    s_bf = _each(lambda a: a.astype(BF16), s)
    vn_bf = _each(lambda a, b: (a[:, :B_HEAD_DIM] - _dot(a[:, B_HEAD_DIM:].astype(BF16), b)).astype(BF16), uw, s_bf)
    o = _each(lambda a, b, m, w: _dot(a, b) + _dot(m.astype(BF16), w), qdec, s_bf, a_qk, vn_bf)
    bpad = jnp.zeros((n, LANES), BF16)
    s_new = _each(lambda a, t, kt, w: a * jnp.exp(t) + _dot(kt, jnp.concatenate([w, bpad], axis=0)), s, gl, kdec_t, vn_bf)
    for (bi, d, h), o_u, s_u in zip(units, o, s_new):
        dirs[d][4][bi, h] = o_u.astype(BF16)
        s_ref[bi, d * B_HEADS + h] = s_u


def _gdn_scan(q, k, v, bg, s0, nseq, lseq):
    nc = lseq // CHUNK
    nb = SCAN_SEQS if nseq % SCAN_SEQS == 0 else 1
    bg3 = bg.reshape(nseq, lseq, LANES)
    hspec_f = pl.BlockSpec((nb, B_HEADS, CHUNK, LANES), lambda b, c: (b, 0, c, 0))
    hspec_b = pl.BlockSpec((nb, B_HEADS, CHUNK, LANES), lambda b, c: (b, 0, nc - 1 - c, 0))
    gspec_f = pl.BlockSpec((nb, CHUNK, LANES), lambda b, c: (b, c, 0))
    gspec_b = pl.BlockSpec((nb, CHUNK, LANES), lambda b, c: (b, nc - 1 - c, 0))
    sspec = pl.BlockSpec((nb, 2 * B_HEADS, B_HEAD_DIM, B_HEAD_DIM), lambda b, c: (b, 0, 0, 0))
    oshape = jax.ShapeDtypeStruct((nseq, B_HEADS, lseq, LANES), BF16)
    return pl.pallas_call(
        _gdn_scan_kernel, grid=(nseq // nb, nc),
        in_specs=[hspec_f, hspec_f, hspec_f, gspec_f, hspec_b, hspec_b, hspec_b, gspec_b, sspec],
        out_specs=[hspec_f, hspec_b, sspec],
        out_shape=[oshape, oshape, jax.ShapeDtypeStruct(s0.shape, F32)],
        compiler_params=_cparams(2, "scan"), name="gdn_scan",
    )(q, k, v, bg3, q, k, v, bg3, s0)


def _gla_level_mats():
    n = CHUNK
    mats = []
    s = n // 2
    while s >= 1:
        m = np.zeros((n, n), np.float32)
        for i in range(n):
            ref = (i // (2 * s)) * 2 * s + s
            if i >= ref:
                m[i, ref + 1:i + 1] = 1.0
            else:
                m[i, i + 1:ref + 1] = -1.0
        mats.append(m)
        s //= 2
    mats.append(np.tril(np.ones((n, n), np.float32)))
    mats.append(np.triu(np.ones((n, n), np.float32), 1))
    fwd = np.concatenate(mats, 0)
    bwd = np.concatenate([m[::-1, ::-1] for m in mats], 0)
    return fwd, bwd


N_LEVELS = int(math.log2(CHUNK))


def _gla_scan_kernel(qf_ref, kf_ref, vf_ref, lowf_ref, qb_ref, kb_ref, vb_ref, lowb_ref, w2_ref, w2lo_ref, gb_ref,
                     mf_ref, mb_ref, s0_ref, of_ref, ob_ref, s_ref):
    c = pl.program_id(1)

    @pl.when(c == 0)
    def _():
        s_ref[...] = s0_ref[...]

    n = CHUNK
    dk_all = C_HEADS * C_KEY_DIM
    ri = lax.broadcasted_iota(jnp.int32, (n, n), 0)
    ci = lax.broadcasted_iota(jnp.int32, (n, n), 1)
    dirs = ((qf_ref, kf_ref, vf_ref, lowf_ref, mf_ref, of_ref, n - 1), (qb_ref, kb_ref, vb_ref, lowb_ref, mb_ref, ob_ref, 0))

    nb = qf_ref.shape[0]
    levs = {}
    for bi in range(nb):
        for d in range(2):
            low_ref, m_ref = dirs[d][3], dirs[d][4]
            cols = slice(d * dk_all, (d + 1) * dk_all)
            low = low_ref[bi]
            low_hi = low.astype(BF16)
            low_lo = (low - low_hi.astype(F32)).astype(BF16)
            logit = (_dot(low_hi, w2_ref[:, cols]) + _dot(low_hi, w2lo_ref[:, cols]) + _dot(low_lo, w2_ref[:, cols])
                     + gb_ref[:, cols])
            log_a = (jnp.minimum(logit, 0.0) - jnp.log1p(jnp.exp(-jnp.abs(logit)))) * (1.0 / C_GATE_TAU)
            levs[bi, d] = _dot_sel(m_ref[...], log_a, parts=2)

    units = [(bi, d, h) for h in range(C_HEADS) for d in range(2) for bi in range(nb)]

    def lev(blk):
        return [levs[bi, d][blk * n:(blk + 1) * n, h * C_KEY_DIM:(h + 1) * C_KEY_DIM] for bi, d, h in units]

    q = [dirs[d][0][bi, h].astype(F32) for bi, d, h in units]
    k = [dirs[d][1][bi, h].astype(F32) for bi, d, h in units]
    v_bf = [dirs[d][2][bi, h] for bi, d, h in units]
    a = _each(lambda x, y: jnp.where(ri == ci, _dot_nt(x.astype(BF16), y.astype(BF16)), 0.0), q, k)
    for lvl in range(N_LEVELS):
        sh = N_LEVELS - 1 - lvl
        same = (ri >> (sh + 1)) == (ci >> (sh + 1))
        hi_r, hi_c = ((ri >> sh) & 1) == 1, ((ci >> sh) & 1) == 1
        pair = (same & hi_r & ~hi_c, same & ~hi_r & hi_c)
        x = _each(lambda t: jnp.exp(-jnp.abs(t)), lev(lvl))
        a = [acc + jnp.where(pair[d], _dot_nt((qq * xx).astype(BF16), (kk * xx).astype(BF16)), 0.0)
             for (_, d, _), acc, qq, kk, xx in zip(units, a, q, k, x)]
    bcum = lev(N_LEVELS)
    rest = lev(N_LEVELS + 1)
    st = [s_ref[bi, d * C_HEADS + h] for bi, d, h in units]
    o = _each(lambda qq, b, s, aa, vv: _dot_nt((qq * jnp.exp(b)).astype(BF16), s.astype(BF16)) + _dot(aa.astype(BF16), vv),
              q, bcum, st, a, v_bf)
    kpad = jnp.zeros((n, C_KEY_DIM), BF16)
    vpad = jnp.zeros((n, C_VAL_DIM), F32)
    kdec = _each(lambda kk, r: jnp.concatenate([(kk * jnp.exp(r)).astype(BF16), kpad], axis=0), k, rest)
    v_t = _each(lambda vv: jnp.concatenate([vv.astype(F32), vpad], axis=0).T.astype(BF16), v_bf)
    s_new = [s * jnp.exp(b[dirs[d][6]:dirs[d][6] + 1, :]) + _dot(vt, kd)
             for (_, d, _), s, b, vt, kd in zip(units, st, bcum, v_t, kdec)]
    for (bi, d, h), o_u, s_u in zip(units, o, s_new):
        dirs[d][5][bi, h] = o_u.astype(BF16)
        s_ref[bi, d * C_HEADS + h] = s_u


def _gla_scan(q, k, v, low, w2, gate_b, mats, s0, nseq, lseq):
    nc = lseq // CHUNK
    nb = SCAN_SEQS if nseq % SCAN_SEQS == 0 else 1
    low3 = low.reshape(nseq, lseq, LANES)
    mf, mb = mats
    w2_hi = w2.astype(BF16)
    w2_lo = (w2 - w2_hi.astype(F32)).astype(BF16)

    def hspec(width, rev):
        return pl.BlockSpec((nb, C_HEADS, CHUNK, width), (lambda b, c: (b, 0, nc - 1 - c, 0)) if rev else (lambda b, c: (b, 0, c, 0)))

    def lspec(rev):
        return pl.BlockSpec((nb, CHUNK, LANES), (lambda b, c: (b, nc - 1 - c, 0)) if rev else (lambda b, c: (b, c, 0)))

    def cspec(shape):
        nd = len(shape)
        return pl.BlockSpec(shape, lambda b, c: (0,) * nd)

    sspec = pl.BlockSpec((nb, 2 * C_HEADS, C_VAL_DIM, C_KEY_DIM), lambda b, c: (b, 0, 0, 0))
    oshape = jax.ShapeDtypeStruct((nseq, C_HEADS, lseq, C_VAL_DIM), BF16)
    return pl.pallas_call(
        _gla_scan_kernel, grid=(nseq // nb, nc),
        in_specs=[hspec(C_KEY_DIM, False), hspec(C_KEY_DIM, False), hspec(C_VAL_DIM, False), lspec(False),
                  hspec(C_KEY_DIM, True), hspec(C_KEY_DIM, True), hspec(C_VAL_DIM, True), lspec(True),
                  cspec(w2.shape), cspec(w2.shape), cspec(gate_b.shape), cspec(mf.shape), cspec(mb.shape), sspec],
        out_specs=[hspec(C_VAL_DIM, False), hspec(C_VAL_DIM, True), sspec],
        out_shape=[oshape, oshape, jax.ShapeDtypeStruct(s0.shape, F32)],
        compiler_params=_cparams(2, "scan"), name="gla_scan",
    )(q, k, v, low3, q, k, v, low3, w2_hi, w2_lo, gate_b, mf, mb, s0)


def _mix_ffn_kernel(*refs, heads, head_dim, tps):
    if heads:
        ofm, ofp, ofn, obm, obp, obn, gm, gp, gn, nw_ref = refs[:10]
        rest = refs[10:]
    else:
        om_ref, op_ref, on_ref = refs[:3]
        rest = refs[3:]
    (wo_ref, x_ref, xp_ref, xn_ref, mod_ref, g0_ref, b0_ref, wu_ref, cw_ref, cb_ref, wd_ref, g1_ref, b1_ref,
     o_ref, inp_ref, xe_ref, h_ref, act_ref) = rest
    t = pl.program_id(0) % tps
    tm = x_ref.shape[0]
    pad = SUBLANES
    main, before, after = slice(pad, pad + tm), slice(0, pad), slice(pad + tm, pad + tm + pad)

    if heads:
        def gated(of_ref, ob_ref, gate_ref):
            parts = []
            for h in range(heads):
                o = of_ref[0, h].astype(F32) + ob_ref[0, h].astype(F32)
                o = o * lax.rsqrt(jnp.mean(o * o, -1, keepdims=True) + NORM_EPS) * nw_ref[...]
                gate = gate_ref[:, h * head_dim:(h + 1) * head_dim].astype(F32)
                parts.append((o * _silu(gate)).astype(BF16))
            return jnp.concatenate(parts, axis=1)

        inp_ref[main, :] = gated(ofm, obm, gm)
        inp_ref[before, :] = gated(ofp, obp, gp)
        inp_ref[after, :] = gated(ofn, obn, gn)
    else:
        inp_ref[main, :] = om_ref[0].astype(F32).T.astype(BF16)
        inp_ref[before, :] = op_ref[0].astype(F32).T[LANES - pad:, :].astype(BF16)
        inp_ref[after, :] = on_ref[0].astype(F32).T[:pad, :].astype(BF16)
    xe_ref[main, :] = x_ref[...]
    xe_ref[before, :] = xp_ref[...]
    xe_ref[after, :] = xn_ref[...]

    y = _dot(inp_ref[...], wo_ref[...])
    xm = _layer_norm(DN_ALPHA * xe_ref[...] + mod_ref[0, 2:3, :] * y, g0_ref[...], b0_ref[...])
    xe_ref[...] = xm
    hm = _modulated(xm, mod_ref, 3)
    h_ref[main, :] = hm[main].astype(BF16)
    h_ref[before, :] = (hm[before] * (t > 0).astype(F32)).astype(BF16)
    h_ref[after, :] = (hm[after] * (t < tps - 1).astype(F32)).astype(BF16)
    for c0 in range(0, D_FF, FF_CHUNK):
        c1 = min(c0 + FF_CHUNK, D_FF)
        val = _dot(h_ref[pad:pad + tm, :], wu_ref[:, c0:c1])
        gx = _dot(h_ref[...], wu_ref[:, D_FF + c0:D_FF + c1])
        gt = (cw_ref[0:1, c0:c1] * pltpu.roll(gx, 1, 0)[pad:pad + tm]
              + cw_ref[1:2, c0:c1] * gx[pad:pad + tm]
              + cw_ref[2:3, c0:c1] * pltpu.roll(gx, tm + 2 * pad - 1, 0)[pad:pad + tm]
              + cb_ref[:, c0:c1])
        act_ref[:, c0:c1] = (_silu(gt) * val).astype(BF16)
    y = _dot(act_ref[...], wd_ref[...])
    o_ref[...] = _layer_norm(DN_ALPHA * xe_ref[main, :] + mod_ref[0, 5:6, :] * y, g1_ref[...], b1_ref[...])


def _mix_ffn(st, w_o, ln_g, ln_b, w_up, conv_w, conv_b, w_down, inp=None, scan_out=None, gate=None, norm_w=None):
    tm, tps = st.tm, st.tps
    r8 = tm // SUBLANES
    nb8 = st.rows // SUBLANES
    sb8 = st.lseq // SUBLANES

    def rows_before(width):
        return pl.BlockSpec((SUBLANES, width), lambda i: (jnp.maximum(i * r8 - 1, 0), 0))

    def rows_after(width):
        return pl.BlockSpec((SUBLANES, width), lambda i: (jnp.minimum((i + 1) * r8, nb8 - 1), 0))

    if scan_out is None:
        heads = head_dim = 0
        feat = inp.shape[1]
        rl = tm // LANES
        nbl = st.lseq // LANES
        in_specs = [pl.BlockSpec((1, feat, tm), lambda i: (i // tps, 0, i % tps)),
                    pl.BlockSpec((1, feat, LANES), lambda i: (i // tps, 0, jnp.maximum((i % tps) * rl - 1, 0))),
                    pl.BlockSpec((1, feat, LANES), lambda i: (i // tps, 0, jnp.minimum((i % tps + 1) * rl, nbl - 1)))]
        args = [inp, inp, inp]
    else:
        o_f, o_b = scan_out
        heads, head_dim = o_f.shape[1], o_f.shape[3]
        hb = pl.BlockSpec((1, heads, SUBLANES, head_dim), lambda i: (i // tps, 0, jnp.maximum((i % tps) * r8 - 1, 0), 0))
        ha = pl.BlockSpec((1, heads, SUBLANES, head_dim),
                          lambda i: (i // tps, 0, jnp.minimum((i % tps + 1) * r8, sb8 - 1), 0))
        hm = st.head_spec(heads, head_dim)
        width = heads * head_dim
        in_specs = [hm, hb, ha, hm, hb, ha, st.row_spec(width), rows_before(width), rows_after(width),
                    _const_spec((1, head_dim))]
        args = [o_f, o_f, o_f, o_b, o_b, o_b, gate, gate, gate, norm_w.reshape(1, -1)]
    in_specs += [_const_spec(w_o.shape), st.row_spec(D_MODEL), rows_before(D_MODEL), rows_after(D_MODEL), st.mod_spec(),
                 _const_spec((1, D_MODEL)), _const_spec((1, D_MODEL)), _const_spec(w_up.shape), _const_spec(conv_w.shape),
                 _const_spec((1, D_FF)), _const_spec(w_down.shape), _const_spec((1, D_MODEL)), _const_spec((1, D_MODEL))]
    args += [w_o, st.x, st.x, st.x, st.mod, ln_g[0].reshape(1, -1), ln_b[0].reshape(1, -1), w_up, conv_w,
             conv_b.reshape(1, -1), w_down, ln_g[1].reshape(1, -1), ln_b[1].reshape(1, -1)]
    ext = tm + 2 * SUBLANES
    return pl.pallas_call(
        functools.partial(_mix_ffn_kernel, heads=heads, head_dim=head_dim, tps=tps), grid=st.grid,
        in_specs=in_specs, out_specs=st.row_spec(D_MODEL),
        out_shape=jax.ShapeDtypeStruct((st.rows, D_MODEL), F32),
        scratch_shapes=[pltpu.VMEM((ext, w_o.shape[0]), BF16), pltpu.VMEM((ext, D_MODEL), F32),
                        pltpu.VMEM((ext, D_MODEL), BF16), pltpu.VMEM((tm, D_FF), BF16)],
        compiler_params=_cparams(1, "mix_ffn"), name="mix_ffn",
    )(*args)


def _rope_tables(n):
    rows = n // GRID_W
    r = jnp.repeat(jnp.arange(rows), GRID_W).astype(F32)
    col = jnp.tile(jnp.arange(GRID_W), rows).astype(F32)
    n_freq = A_HEAD_DIM // 4
    inv = jnp.power(ROPE_BASE, -jnp.arange(n_freq, dtype=F32) / n_freq)
    ang = jnp.concatenate([r[:, None] * inv, col[:, None] * inv], -1)
    cos, sin = jnp.cos(ang), jnp.sin(ang)
    zero = jnp.zeros_like(sin)
    reps = LANES // A_HEAD_DIM
    c = jnp.tile(jnp.concatenate([cos, cos], -1), (1, reps))
    s1 = jnp.tile(jnp.concatenate([-sin, zero], -1), (1, reps))
    s2 = jnp.tile(jnp.concatenate([zero, sin], -1), (1, reps))
    return c, s1, s2


def _pad_cols(w, width):
    return jnp.pad(w, ((0, 0), (0, width - w.shape[1])))


def kernel(x, c, ctx, c_ctx, ada_w, ada_b, ln_g, ln_b, ffn_w_up, ffn_conv_w, ffn_conv_b, ffn_w_down, attn_w_qkv, attn_sink, attn_w_o, gdn_w_in, gdn_conv_w, gdn_a_log, gdn_dt_bias, gdn_norm_w, gdn_w_o, gla_w_in, gla_w_gate2, gla_gate_b, gla_norm_w, gla_w_o):
    bsz, n, d = x.shape
    nctx = ctx.shape[1]
    assert d == D_MODEL and n % A_BLOCK == 0 and n % GRID_W == 0 and nctx % CHUNK == 0 and n % CHUNK == 0
    assert A_WINDOW == A_BLOCK and nctx % A_BLOCK == 0

    cond_rows = -(-(bsz + 1) // SUBLANES) * SUBLANES
    cond = jnp.zeros((cond_rows, d), F32).at[:bsz].set(c).at[bsz].set(c_ctx)
    mod_all = _ada(cond, ada_w, ada_b)

    rope_tabs = _rope_tables(n)
    gla_mats = tuple(jnp.asarray(m, BF16) for m in _gla_level_mats())

    xl = x.reshape(bsz * n, d)
    xc = ctx.reshape(bsz * nctx, d)
    for i in range(DEPTH):
        need_ctx_out = i < DEPTH - 1
        lat = _Stream(xl, mod_all[i, :bsz].reshape(bsz, 6, d), bsz, n)
        cx = _Stream(xc, mod_all[i, bsz:bsz + 1].reshape(1, 6, d), bsz, nctx)
        kind, slot = i % N_MIXERS, i // N_MIXERS
        if kind == 0:
            w_qkv = attn_w_qkv[slot].astype(BF16)
            w_o = attn_w_o[slot].astype(BF16)
            qt_c, k_c, vt_c = _attn_proj(cx, w_qkv, None)
            qt_l, k_l, vt_l = _attn_proj(lat, w_qkv, rope_tabs)
            mix_l = dict(inp=_attention(qt_l, k_l, vt_l, k_c, vt_c, attn_sink[slot], bsz, n, nctx, True))
            if need_ctx_out:
                mix_c = dict(inp=_attention(qt_c, None, None, k_c, vt_c, attn_sink[slot], bsz, nctx, nctx, False))
        elif kind == 1:
            hk = B_HEADS * B_HEAD_DIM
            w_in = gdn_w_in[slot]
            w_main = w_in[:, :4 * hk].astype(BF16)
            w_small = _pad_cols(w_in[:, 4 * hk:], LANES).astype(BF16)
            w_o = gdn_w_o[slot].astype(BF16)
            a_vec = jnp.zeros((1, LANES), F32).at[0, 2 * B_HEADS:4 * B_HEADS].set(jnp.exp(gdn_a_log[slot]).reshape(-1))
            dt_vec = jnp.zeros((1, LANES), F32).at[0, 2 * B_HEADS:4 * B_HEADS].set(gdn_dt_bias[slot].reshape(-1))
            state = jnp.zeros((bsz, 2 * B_HEADS, B_HEAD_DIM, B_HEAD_DIM), F32)
            outs = []
            for st in (cx, lat):
                q, k, v, gate, bg = _gdn_proj(st, w_main, w_small, gdn_conv_w[slot], a_vec, dt_vec)
                o_f, o_b, state = _gdn_scan(q, k, v, bg, state, st.nseq, st.lseq)
                outs.append(dict(scan_out=(o_f, o_b), gate=gate, norm_w=gdn_norm_w[slot]))
            mix_c, mix_l = outs
        else:
            dk, dv = C_HEADS * C_KEY_DIM, C_HEADS * C_VAL_DIM
            w_in = gla_w_in[slot]
            w_main = w_in[:, :2 * dk + 2 * dv].astype(BF16)
            w_small = _pad_cols(w_in[:, 2 * dk + 2 * dv:], LANES).astype(BF16)
            w_o = gla_w_o[slot].astype(BF16)
            w2 = jnp.zeros((LANES, 2 * dk), F32)
            for z in range(2):
                w2 = w2.at[z * C_GATE_RANK:(z + 1) * C_GATE_RANK, z * dk:(z + 1) * dk].set(gla_w_gate2[slot, z])
            gate_b = gla_gate_b[slot].reshape(1, 2 * dk)
            state = jnp.zeros((bsz, 2 * C_HEADS, C_VAL_DIM, C_KEY_DIM), F32)
            outs = []
            for st in (cx, lat):
                q, k, v, gate, low = _gla_proj(st, w_main, w_small)
                o_f, o_b, state = _gla_scan(q, k, v, low, w2, gate_b, gla_mats, state, st.nseq, st.lseq)
                outs.append(dict(scan_out=(o_f, o_b), gate=gate, norm_w=gla_norm_w[slot]))
            mix_c, mix_l = outs

        w_up = ffn_w_up[i].astype(BF16)
        w_down = ffn_w_down[i].astype(BF16)
        streams = [(lat, mix_l)] + ([(cx, mix_c)] if need_ctx_out else [])
        new = []
        for st, mix in streams:
            new.append(_mix_ffn(st, w_o, ln_g[i], ln_b[i], w_up, ffn_conv_w[i], ffn_conv_b[i], w_down, **mix))
        xl = new[0]
        if need_ctx_out:
            xc = new[1]
    return xl.reshape(bsz, n, d)
```
